```python
import jax, jax.numpy as jnp
from jax import lax
import numpy as np

D_MODEL = 1024
BATCH = 32
SEQ = 2048
DEPTH = 1

N_HEADS = 16
N_KV_HEADS = 4
HEAD_DIM = 64
D_ATTN = N_HEADS * HEAD_DIM
D_KV = N_KV_HEADS * HEAD_DIM
WINDOW = 128
BLOCK = 128
KSPAN = BLOCK + 2 * WINDOW
NEG_INF = -1e30
D_CONV = D_MODEL
CONV_WIDTH = 3
N_EXPERTS = 16
CAPACITY_FACTOR = 2
D_FF_EXPERT = 2048
DEEPNORM_ALPHA = (2.0 * DEPTH) ** 0.25
DEEPNORM_BETA = (8.0 * DEPTH) ** -0.25
LN_EPS = 1e-5
IN_SPLITS = (D_ATTN, D_KV, D_KV, D_CONV, D_CONV, D_CONV, D_MODEL, D_MODEL)
D_IN = sum(IN_SPLITS)
SPLIT_IDX = [int(c) for c in np.cumsum(IN_SPLITS)[:-1]]

kernel_name = "hybrid_conv_swa_ec_moe_deepnorm"


def layer_norm(x, g, b):
    xf = x.astype(jnp.float32)
    mu = xf.mean(-1, keepdims=True)
    var = jnp.square(xf - mu).mean(-1, keepdims=True)
    y = (xf - mu) * lax.rsqrt(var + LN_EPS) * g.astype(jnp.float32) + b.astype(jnp.float32)
    return y.astype(x.dtype)


def alibi_slopes(n_heads):
    return (2.0 ** (-8.0 * np.arange(1, n_heads + 1) / n_heads)).astype(np.float32)


def windowed_gqa(q, k, v, sink):
    B, S, _ = q.shape
    nb = S // BLOCK
    rep = N_HEADS // N_KV_HEADS
    qb = q.reshape(B, nb, BLOCK, N_KV_HEADS, rep, HEAD_DIM).astype(jnp.float32) * (HEAD_DIM ** -0.5)
    qb = jnp.moveaxis(qb, 1, 0)
    pad = ((0, 0), (WINDOW, WINDOW), (0, 0), (0, 0))
    kp = jnp.pad(k.reshape(B, S, N_KV_HEADS, HEAD_DIM).astype(jnp.float32), pad)
    vp = jnp.pad(v.reshape(B, S, N_KV_HEADS, HEAD_DIM).astype(jnp.float32), pad)
    slopes = jnp.asarray(alibi_slopes(N_HEADS)).reshape(N_KV_HEADS, rep)
    i = jnp.arange(BLOCK)[:, None]
    j = jnp.arange(KSPAN)[None, :]
    dist = jnp.abs(i - j + WINDOW)
    bias = -slopes[:, :, None, None] * dist.astype(jnp.float32)
    sink = sink.astype(jnp.float32).reshape(N_KV_HEADS, rep)[None, :, :, None]

    def one_block(args):
        q_blk, n = args
        start = n * BLOCK
        k_blk = lax.dynamic_slice_in_dim(kp, start, KSPAN, axis=1)
        v_blk = lax.dynamic_slice_in_dim(vp, start, KSPAN, axis=1)
        kpos = start - WINDOW + j
        valid = (dist <= WINDOW) & (kpos >= 0) & (kpos < S)
        s = jnp.einsum('bqgrd,bkgd->bgrqk', q_blk, k_blk) + bias
        s = jnp.where(valid, s, NEG_INF)
        m = jnp.maximum(s.max(-1), sink)
        p = jnp.exp(s - m[..., None])
        denom = p.sum(-1) + jnp.exp(sink - m)
        return jnp.einsum('bgrqk,bkgd->bqgrd', p / denom[..., None], v_blk)

    o = lax.map(one_block, (qb, jnp.arange(nb)))
    return jnp.moveaxis(o, 0, 1).reshape(B, S, D_ATTN).astype(q.dtype)


def short_conv(b_gate, c_gate, xc, conv_w):
    u = c_gate * xc
    S = u.shape[1]
    up = jnp.pad(u, ((0, 0), (CONV_WIDTH // 2, CONV_WIDTH // 2), (0, 0)))
    y = conv_w[0] * up[:, 0:S]
    for w in range(1, CONV_WIDTH):
        y = y + conv_w[w] * up[:, w:w + S]
    return b_gate * y


def hybrid_mixer(x, w_in, conv_w, attn_sink, w_attn_o, w_conv_o, w_out):
    proj = x @ w_in
    q, k, v, cb, cc, cx, ga, gc = jnp.split(proj, SPLIT_IDX, axis=-1)
    y_attn = windowed_gqa(q, k, v, attn_sink) @ w_attn_o
    y_conv = short_conv(cb, cc, cx, conv_w) @ w_conv_o
    merged = jax.nn.sigmoid(ga) * y_attn + jax.nn.sigmoid(gc) * y_conv
    return merged @ w_out


def expert_choice_moe(x, w_router, w_gate, w_up, w_down):
    B, S, _ = x.shape
    cap = CAPACITY_FACTOR * S // N_EXPERTS
    logits = jnp.einsum('bsd,de->bse', x, w_router).astype(jnp.float32)
    aff = jax.nn.softmax(logits, axis=-1)
    top_val, top_idx = lax.top_k(jnp.swapaxes(aff, 1, 2), cap)
    b_idx = jnp.arange(B)[:, None, None]
    xe = x[b_idx, top_idx]
    h = jax.nn.silu(jnp.einsum('becd,edf->becf', xe, w_gate)) * jnp.einsum('becd,edf->becf', xe, w_up)
    ye = jnp.einsum('becf,efd->becd', h, w_down) * top_val[..., None].astype(x.dtype)
    return jnp.zeros_like(x).at[b_idx, top_idx].add(ye)


def setup_inputs(seed: int = 0) -> dict:
    key = jax.random.key(seed)
    ks = jax.random.split(key, 18)
    f32 = jnp.float32
    nrm = lambda k, s: jax.random.normal(k, s, f32)
    L = DEPTH
    col_scale = np.ones((D_IN,), np.float32)
    col_scale[D_ATTN + D_KV:D_ATTN + 2 * D_KV] = DEEPNORM_BETA
    w_in = nrm(ks[3], (L, D_MODEL, D_IN)) * (D_MODEL ** -0.5) * jnp.asarray(col_scale)
    return {
        "x": nrm(ks[0], (BATCH, SEQ, D_MODEL)),
        "ln0_g": 1.0 + 0.02 * nrm(ks[1], (D_MODEL,)),
        "ln0_b": 0.02 * nrm(ks[2], (D_MODEL,)),
        "w_in": w_in,
        "conv_w": nrm(ks[4], (L, CONV_WIDTH, D_CONV)) * (CONV_WIDTH ** -0.5),
        "attn_sink": 0.5 * nrm(ks[5], (L, N_HEADS)),
        "w_attn_o": nrm(ks[6], (L, D_ATTN, D_MODEL)) * (D_ATTN ** -0.5) * DEEPNORM_BETA,
        "w_conv_o": nrm(ks[7], (L, D_CONV, D_MODEL)) * (D_CONV ** -0.5) * DEEPNORM_BETA,
        "w_out": nrm(ks[8], (L, D_MODEL, D_MODEL)) * (D_MODEL ** -0.5) * DEEPNORM_BETA,
        "ln1_g": 1.0 + 0.02 * nrm(ks[9], (L, D_MODEL)),
        "ln1_b": 0.02 * nrm(ks[10], (L, D_MODEL)),
        "w_router": nrm(ks[11], (L, D_MODEL, N_EXPERTS)) * (D_MODEL ** -0.5),
        "w_gate": nrm(ks[12], (L, N_EXPERTS, D_MODEL, D_FF_EXPERT)) * (D_MODEL ** -0.5),
        "w_up": nrm(ks[13], (L, N_EXPERTS, D_MODEL, D_FF_EXPERT)) * (D_MODEL ** -0.5),
        "w_down": nrm(ks[14], (L, N_EXPERTS, D_FF_EXPERT, D_MODEL)) * (D_FF_EXPERT ** -0.5) * DEEPNORM_BETA,
        "ln2_g": 1.0 + 0.02 * nrm(ks[15], (L, D_MODEL)),
        "ln2_b": 0.02 * nrm(ks[16], (L, D_MODEL)),
    }


def reference(x, ln0_g, ln0_b, w_in, conv_w, attn_sink, w_attn_o, w_conv_o, w_out,
              ln1_g, ln1_b, w_router, w_gate, w_up, w_down, ln2_g, ln2_b):
    x = layer_norm(x, ln0_g, ln0_b)
    for l in range(DEPTH):
        h = hybrid_mixer(x, w_in[l], conv_w[l], attn_sink[l], w_attn_o[l], w_conv_o[l], w_out[l])
        x = layer_norm(DEEPNORM_ALPHA * x + h, ln1_g[l], ln1_b[l])
        f = expert_choice_moe(x, w_router[l], w_gate[l], w_up[l], w_down[l])
        x = layer_norm(DEEPNORM_ALPHA * x + f, ln2_g[l], ln2_b[l])
    return x
```

```python
import functools

import jax
import jax.numpy as jnp
import numpy as np
from jax import lax
from jax.experimental import pallas as pl
from jax.experimental.pallas import tpu as pltpu

F32 = jnp.float32
BF16 = jnp.bfloat16

N_HEADS = 16
N_KV_HEADS = 4
HEAD_DIM = 64
REP = N_HEADS // N_KV_HEADS
WINDOW = 128
BLOCK = 128
NEG_INF = -1e30
CONV_WIDTH = 3
N_EXPERTS = 16
CAPACITY_FACTOR = 2
DEPTH = 1
DEEPNORM_ALPHA = (2.0 * DEPTH) ** 0.25
LN_EPS = 1e-5

V7X_VMEM_BYTES = 64 * 1024 * 1024
VMEM_LIMIT = 56 * 1024 * 1024

TOKEN_TILE = 512
SUB = BLOCK


def _layer_norm(x, g, b):
    mu = jnp.mean(x, axis=-1, keepdims=True)
    xc = x - mu
    var = jnp.mean(xc * xc, axis=-1, keepdims=True)
    return xc * lax.rsqrt(var + LN_EPS) * g + b


def _params(sem):
    return pltpu.CompilerParams(dimension_semantics=sem, vmem_limit_bytes=VMEM_LIMIT)


def _const_spec(shape):
    zeros = (0,) * len(shape)
    return pl.BlockSpec(shape, lambda *_: zeros, pipeline_mode=pl.Buffered(1))


def _inproj_body(splits, x_ref, g_ref, b_ref, w_ref, xn_ref, *out_refs):
    xn = _layer_norm(x_ref[...], g_ref[...], b_ref[...])
    xn_ref[...] = xn
    xb = xn.astype(BF16)
    off = 0
    for (width, scale), o_ref in zip(splits, out_refs):
        chunk = min(width, 512)
        for c in range(0, width, chunk):
            acc = jnp.dot(xb, w_ref[:, off + c:off + c + chunk], preferred_element_type=F32)
            if scale != 1.0:
                acc = acc * scale
            o_ref[:, c:c + chunk] = acc.astype(BF16)
        off += width


def _inproj(x2, g, b, w_bf16, splits):
    T, D = x2.shape
    n_in = w_bf16.shape[1]
    tm = TOKEN_TILE
    out_shape = [jax.ShapeDtypeStruct((T, D), F32)]
    out_specs = [pl.BlockSpec((tm, D), lambda i: (i, 0))]
    for width, _ in splits:
        out_shape.append(jax.ShapeDtypeStruct((T, width), BF16))
        out_specs.append(pl.BlockSpec((tm, width), lambda i: (i, 0)))
    return pl.pallas_call(
        functools.partial(_inproj_body, splits),
        grid=(T // tm,),
        in_specs=[
            pl.BlockSpec((tm, D), lambda i: (i, 0)),
            _const_spec((1, D)),
            _const_spec((1, D)),
            _const_spec((D, n_in)),
        ],
        out_specs=out_specs,
        out_shape=out_shape,
        compiler_params=_params(("parallel",)),
        name="inproj",
    )(x2, g, b, w_bf16)


def _alibi_slopes():
    return (2.0 ** (-8.0 * np.arange(1, N_HEADS + 1) / N_HEADS)).astype(np.float32)


def _attention_bias():
    slopes = _alibi_slopes().reshape(N_KV_HEADS, REP)
    i = np.arange(SUB)[:, None]
    j = np.arange(SUB)[None, :]
    out = np.empty((5, N_KV_HEADS, REP * SUB, SUB), np.float32)
    for var, (shift, masked) in enumerate([(-SUB, False), (-SUB, True), (0, False), (SUB, False), (SUB, True)]):
        dist = np.abs(i - (j + shift))
        valid = (dist <= WINDOW) & (not masked)
        for g in range(N_KV_HEADS):
            for r in range(REP):
                b = (-slopes[g, r] * dist.astype(np.float32)).astype(np.float32)
                out[var, g, r * SUB:(r + 1) * SUB] = np.where(valid, b, np.float32(NEG_INF))
    return out


def _attention_body(n_sub, sink_ref, q_ref, kp_ref, kc_ref, kn_ref, vp_ref, vc_ref, vn_ref,
                    bias_ref, o_ref):
    n = pl.program_id(1)
    first = n == 0
    last = n == pl.num_programs(1) - 1
    kwin = jnp.concatenate([kp_ref[...], kc_ref[...], kn_ref[...]], axis=0)
    vwin = jnp.concatenate([vp_ref[...], vc_ref[...], vn_ref[...]], axis=0)
    for j in range(n_sub):
        qj = q_ref[j * SUB:(j + 1) * SUB, :]
        prev_var = jnp.where(first, 1, 0) if j == 0 else 0
        next_var = jnp.where(last, 4, 3) if j == n_sub - 1 else 3
        heads = [None] * N_HEADS
        for g in range(N_KV_HEADS):
            qg = jnp.concatenate(
                [qj[:, (g * REP + r) * HEAD_DIM:(g * REP + r + 1) * HEAD_DIM] for r in range(REP)], axis=0)
            kg = kwin[j * SUB:(j + 3) * SUB, g * HEAD_DIM:(g + 1) * HEAD_DIM]
            vg = vwin[j * SUB:(j + 3) * SUB, g * HEAD_DIM:(g + 1) * HEAD_DIM]
            s = lax.dot_general(qg, kg, (((1,), (1,)), ((), ())), preferred_element_type=F32)
            bias = jnp.concatenate([bias_ref[prev_var, g], bias_ref[2, g], bias_ref[next_var, g]], axis=1)
            s = s + bias
            sink = jnp.concatenate(
                [jnp.full((SUB, 1), sink_ref[g * REP + r], F32) for r in range(REP)], axis=0)
            m = jnp.maximum(jnp.max(s, axis=1, keepdims=True), sink)
            p = jnp.exp(s - m)
            denom = jnp.sum(p, axis=1, keepdims=True) + jnp.exp(sink - m)
            og = jnp.dot(p.astype(BF16), vg, preferred_element_type=F32) / denom
            for r in range(REP):
                heads[g * REP + r] = og[r * SUB:(r + 1) * SUB, :]
        o_ref[j * SUB:(j + 1) * SUB, :] = jnp.concatenate(heads, axis=1).astype(BF16)


def _attention(q, k, v, sink, B, S):
    T, d_attn = q.shape
    d_kv = k.shape[1]
    tq = TOKEN_TILE
    n_sub = tq // SUB
    nq = S // tq
    nblk = S // SUB
    bias = jnp.asarray(_attention_bias())

    def q_map(b, n):
        return (b * nq + n, 0)

    def prev_map(b, n):
        return (b * nblk + jnp.maximum(n * n_sub - 1, 0), 0)

    def next_map(b, n):
        return (b * nblk + jnp.minimum((n + 1) * n_sub, nblk - 1), 0)

    kv_specs = [pl.BlockSpec((SUB, d_kv), prev_map), pl.BlockSpec((tq, d_kv), q_map),
                pl.BlockSpec((SUB, d_kv), next_map)]
    return pl.pallas_call(
        functools.partial(_attention_body, n_sub),
        grid=(B, nq),
        in_specs=[pl.BlockSpec(memory_space=pltpu.SMEM),
                  pl.BlockSpec((tq, d_attn), q_map)] + kv_specs + kv_specs + [_const_spec(bias.shape)],
        out_specs=pl.BlockSpec((tq, d_attn), q_map),
        out_shape=jax.ShapeDtypeStruct((T, d_attn), BF16),
        compiler_params=_params(("parallel", "parallel")),
        name="attention",
    )(sink, q, k, k, k, v, v, v, bias)


HALO = 16


def _mixer_out_body(tiles_per_seq, o_ref, cb_ref, cc_ref, cx_ref, ccp_ref, cxp_ref, ccn_ref, cxn_ref,
                    ga_ref, gc_ref, xn_ref, cw_ref, wa_ref, wc_ref, wo_ref, g_ref, b_ref, wr_ref,
                    x1_ref, x1b_ref, lg_ref):
    i = pl.program_id(0)
    n = i % tiles_per_seq
    tm = cc_ref.shape[0]
    u = cc_ref[...].astype(F32) * cx_ref[...].astype(F32)
    u_before = ccp_ref[HALO - 1:HALO, :].astype(F32) * cxp_ref[HALO - 1:HALO, :].astype(F32)
    u_after = ccn_ref[0:1, :].astype(F32) * cxn_ref[0:1, :].astype(F32)
    u_before = jnp.where(n == 0, 0.0, u_before)
    u_after = jnp.where(n == tiles_per_seq - 1, 0.0, u_after)
    row = lax.broadcasted_iota(jnp.int32, u.shape, 0)
    um1 = jnp.where(row == 0, u_before, pltpu.roll(u, 1, 0))
    up1 = jnp.where(row == tm - 1, u_after, pltpu.roll(u, tm - 1, 0))
    cw = cw_ref[...]
    y = cb_ref[...].astype(F32) * (cw[0:1, :] * um1 + cw[1:2, :] * u + cw[2:3, :] * up1)
    y_conv = jnp.dot(y.astype(BF16), wc_ref[...], preferred_element_type=F32)
    y_attn = jnp.dot(o_ref[...], wa_ref[...], preferred_element_type=F32)
    merged = (jax.nn.sigmoid(ga_ref[...].astype(F32)) * y_attn
              + jax.nn.sigmoid(gc_ref[...].astype(F32)) * y_conv)
    h = jnp.dot(merged.astype(BF16), wo_ref[...], preferred_element_type=F32)
    x1 = _layer_norm(DEEPNORM_ALPHA * xn_ref[...] + h, g_ref[...], b_ref[...])
    x1_ref[...] = x1
    x1b_ref[...] = x1.astype(BF16)
    lg_ref[...] = lax.dot_general(wr_ref[...], x1, (((1,), (1,)), ((), ())),
                                  precision=lax.Precision.HIGHEST, preferred_element_type=F32)


def _mixer_out(o, cb, cc, cx, ga, gc, xn, conv_w, wa, wc, wo, g, b, wr_t, S):
    T, D = xn.shape
    tm = TOKEN_TILE
    tiles_per_seq = S // tm
    hb = tm // HALO
    n_halo = T // HALO
    E = wr_t.shape[0]

    def tile(i):
        return (i, 0)

    def before(i):
        return (jnp.maximum(i * hb - 1, 0), 0)

    def after(i):
        return (jnp.minimum((i + 1) * hb, n_halo - 1), 0)

    tspec = pl.BlockSpec((tm, D), tile)
    return pl.pallas_call(
        functools.partial(_mixer_out_body, tiles_per_seq),
        grid=(T // tm,),
        in_specs=[tspec, tspec, tspec, tspec,
                  pl.BlockSpec((HALO, D), before), pl.BlockSpec((HALO, D), before),
                  pl.BlockSpec((HALO, D), after), pl.BlockSpec((HALO, D), after),
                  tspec, tspec, tspec,
                  _const_spec(conv_w.shape), _const_spec(wa.shape), _const_spec(wc.shape),
                  _const_spec(wo.shape), _const_spec(g.shape), _const_spec(b.shape),
                  _const_spec(wr_t.shape)],
        out_specs=[tspec, tspec, pl.BlockSpec((E, tm), lambda i: (0, i))],
        out_shape=[jax.ShapeDtypeStruct((T, D), F32), jax.ShapeDtypeStruct((T, D), BF16),
                   jax.ShapeDtypeStruct((E, T), F32)],
        compiler_params=_params(("parallel",)),
        name="mixer_out",
    )(o, cb, cc, cx, cc, cx, cc, cx, ga, gc, xn, conv_w, wa, wc, wo, g, b, wr_t)


SCAN = 256


def _exclusive_count(mask, tri):
    E, S = mask.shape
    mb = mask.astype(BF16)
    carry = jnp.zeros((E, 1), F32)
    parts = []
    for c in range(0, S, SCAN):
        chunk = mb[:, c:c + SCAN]
        parts.append(jnp.dot(chunk, tri, preferred_element_type=F32) + carry)
        carry = carry + jnp.sum(chunk.astype(F32), axis=1, keepdims=True)
    return jnp.concatenate(parts, axis=1)


def _route_body(cap, lg_ref, tri_ref, slot_ref, aff_ref):
    logits = lg_ref[...]
    mx = jnp.max(logits, axis=0, keepdims=True)
    ex = jnp.exp(logits - mx)
    aff = ex / jnp.sum(ex, axis=0, keepdims=True)
    aff_ref[...] = aff
    bits = pltpu.bitcast(aff, jnp.int32)
    E = bits.shape[0]

    def step(k, prefix):
        cand = prefix | lax.shift_left(jnp.int32(1), 30 - k)
        cnt = jnp.sum((bits >= cand).astype(F32), axis=1, keepdims=True)
        return jnp.where(cnt >= cap, cand, prefix)

    thresh = lax.fori_loop(0, 31, step, jnp.zeros((E, 1), jnp.int32))
    gt = bits > thresh
    eq = bits == thresh
    need = cap - jnp.sum(gt.astype(F32), axis=1, keepdims=True)
    tri = tri_ref[...]
    sel = gt | (eq & (_exclusive_count(eq, tri) < need))
    pos = _exclusive_count(sel, tri)
    slot_ref[...] = jnp.where(sel, pos, -1.0).astype(jnp.int32)


def _route(logits_t, B, S, cap):
    E = logits_t.shape[0]
    tri = jnp.asarray(np.triu(np.ones((SCAN, SCAN), np.float32), 1), BF16)
    spec = pl.BlockSpec((E, S), lambda b: (0, b))
    return pl.pallas_call(
        functools.partial(_route_body, cap),
        grid=(B,),
        in_specs=[spec, _const_spec(tri.shape)],
        out_specs=[spec, spec],
        out_shape=[jax.ShapeDtypeStruct((E, B * S), jnp.int32), jax.ShapeDtypeStruct((E, B * S), F32)],
        compiler_params=_params(("parallel",)),
        name="route",
    )(logits_t, tri)


def _experts_body(cap, slot_ref, aff_ref, x_ref, wg_ref, wu_ref, wd_ref, y_ref):
    slot = slot_ref[0]
    aff = aff_ref[0]
    S = slot.shape[1]
    hit = lax.broadcasted_iota(jnp.int32, (cap, S), 0) == slot
    onehot = jnp.where(hit, 1.0, 0.0).astype(BF16)
    xe = jnp.dot(onehot, x_ref[...], preferred_element_type=F32).astype(BF16)
    top_val = jnp.sum(jnp.where(hit, aff, 0.0), axis=1, keepdims=True)
    gate = jnp.dot(xe, wg_ref[0], preferred_element_type=F32)
    up = jnp.dot(xe, wu_ref[0], preferred_element_type=F32)
    h = (gate * jax.nn.sigmoid(gate) * up).astype(BF16)
    ye = jnp.dot(h, wd_ref[0], preferred_element_type=F32) * top_val
    y_ref[0, 0] = ye.astype(BF16)


def _experts(slot, aff, x1b, wg, wu, wd, B, S, cap):
    E, D, F = wg.shape
    slot3 = slot.reshape(E, 1, B * S)
    aff3 = aff.reshape(E, 1, B * S)
    row = pl.BlockSpec((1, 1, S), lambda e, b: (e, 0, b))
    return pl.pallas_call(
        functools.partial(_experts_body, cap),
        grid=(E, B),
        in_specs=[row, row,
                  pl.BlockSpec((S, D), lambda e, b: (b, 0)),
                  pl.BlockSpec((1, D, F), lambda e, b: (e, 0, 0), pipeline_mode=pl.Buffered(1)),
                  pl.BlockSpec((1, D, F), lambda e, b: (e, 0, 0), pipeline_mode=pl.Buffered(1)),
                  pl.BlockSpec((1, F, D), lambda e, b: (e, 0, 0), pipeline_mode=pl.Buffered(1))],
        out_specs=pl.BlockSpec((1, 1, cap, D), lambda e, b: (b, e, 0, 0)),
        out_shape=jax.ShapeDtypeStruct((B, E, cap, D), BF16),
        compiler_params=_params(("arbitrary", "arbitrary")),
        name="experts",
    )(slot3, aff3, x1b, wg, wu, wd)


def _combine_body(cap, slot_ref, y_ref, x1_ref, g_ref, b_ref, o_ref):
    E = slot_ref.shape[0]
    tm = slot_ref.shape[1]
    ci = lax.broadcasted_iota(jnp.int32, (cap, tm), 0)
    pieces = [jnp.where(ci == slot_ref[e:e + 1, :], 1.0, 0.0).astype(BF16) for e in range(E)]
    scatter = jnp.concatenate(pieces, axis=0)
    f = lax.dot_general(scatter, y_ref[0], (((0,), (0,)), ((), ())), preferred_element_type=F32)
    o_ref[...] = _layer_norm(DEEPNORM_ALPHA * x1_ref[...] + f, g_ref[...], b_ref[...])


def _combine(slot, y, x1, g, b, B, S, cap):
    T, D = x1.shape
    E = slot.shape[0]
    tm = TOKEN_TILE
    tiles_per_seq = S // tm
    y2 = y.reshape(B, E * cap, D)
    return pl.pallas_call(
        functools.partial(_combine_body, cap),
        grid=(T // tm,),
        in_specs=[pl.BlockSpec((E, tm), lambda i: (0, i)),
                  pl.BlockSpec((1, E * cap, D), lambda i: (i // tiles_per_seq, 0, 0)),
                  pl.BlockSpec((tm, D), lambda i: (i, 0)),
                  _const_spec(g.shape), _const_spec(b.shape)],
        out_specs=pl.BlockSpec((tm, D), lambda i: (i, 0)),
        out_shape=jax.ShapeDtypeStruct((T, D), F32),
        compiler_params=_params(("parallel",)),
        name="combine",
    )(slot, y2, x1, g, b)


def kernel(x, ln0_g, ln0_b, w_in, conv_w, attn_sink, w_attn_o, w_conv_o, w_out, ln1_g, ln1_b,
           w_router, w_gate, w_up, w_down, ln2_g, ln2_b):
    B, S, D = x.shape
    assert w_in.shape[0] == DEPTH == 1
    assert S % TOKEN_TILE == 0 and TOKEN_TILE % SUB == 0
    T = B * S
    d_attn = N_HEADS * HEAD_DIM
    d_kv = N_KV_HEADS * HEAD_DIM
    cap = CAPACITY_FACTOR * S // N_EXPERTS
    row = lambda a: a.reshape(1, -1).astype(F32)

    splits = ((d_attn, HEAD_DIM ** -0.5), (d_kv, 1.0), (d_kv, 1.0),
              (D, 1.0), (D, 1.0), (D, 1.0), (D, 1.0), (D, 1.0))
    xn, q, k, v, cb, cc, cx, ga, gc = _inproj(
        x.reshape(T, D), row(ln0_g), row(ln0_b), w_in[0].astype(BF16), splits)

    o = _attention(q, k, v, attn_sink[0].astype(F32), B, S)

    x1, x1b, logits_t = _mixer_out(
        o, cb, cc, cx, ga, gc, xn, conv_w[0].astype(F32),
        w_attn_o[0].astype(BF16), w_conv_o[0].astype(BF16), w_out[0].astype(BF16),
        row(ln1_g[0]), row(ln1_b[0]), w_router[0].T.astype(F32), S)

    slot, aff = _route(logits_t, B, S, cap)

    y = _experts(slot, aff, x1b, w_gate[0].astype(BF16), w_up[0].astype(BF16),
                 w_down[0].astype(BF16), B, S, cap)

    out = _combine(slot, y, x1, row(ln2_g[0]), row(ln2_b[0]), B, S, cap)
    return out.reshape(B, S, D)
```

```python
import functools

import jax
import jax.numpy as jnp
import numpy as np
from jax import lax
from jax.experimental import pallas as pl
from jax.experimental.pallas import tpu as pltpu

F32 = jnp.float32
BF16 = jnp.bfloat16

N_HEADS = 16
N_KV_HEADS = 4
HEAD_DIM = 64
REP = N_HEADS // N_KV_HEADS
WINDOW = 128
BLOCK = 128
NEG_INF = -1e30
LOG2E = float(np.log2(np.e))
CONV_WIDTH = 3
N_EXPERTS = 16
CAPACITY_FACTOR = 2
DEPTH = 1
DEEPNORM_ALPHA = (2.0 * DEPTH) ** 0.25
LN_EPS = 1e-5

V7X_VMEM_BYTES = 64 * 1024 * 1024
VMEM_LIMIT = 56 * 1024 * 1024

TOKEN_TILE = 512
SUB = BLOCK


def _layer_norm(x, g, b):
    mu = jnp.mean(x, axis=-1, keepdims=True)
    xc = x - mu
    var = jnp.mean(xc * xc, axis=-1, keepdims=True)
    return xc * lax.rsqrt(var + LN_EPS) * g + b


def _params(sem):
    return pltpu.CompilerParams(dimension_semantics=sem, vmem_limit_bytes=VMEM_LIMIT)


def _const_spec(shape):
    zeros = (0,) * len(shape)
    return pl.BlockSpec(shape, lambda *_: zeros, pipeline_mode=pl.Buffered(1))


def _inproj_body(splits, x_ref, g_ref, b_ref, w_ref, xn_ref, *out_refs):
    xn = _layer_norm(x_ref[...], g_ref[...], b_ref[...])
    xn_ref[...] = xn
    xb = xn.astype(BF16)
    off = 0
    for (width, scale), o_ref in zip(splits, out_refs):
        chunk = min(width, 512)
        for c in range(0, width, chunk):
            acc = jnp.dot(xb, w_ref[:, off + c:off + c + chunk], preferred_element_type=F32)
            if scale != 1.0:
                acc = acc * scale
            o_ref[:, c:c + chunk] = acc.astype(BF16)
        off += width


def _inproj(x2, g, b, w_bf16, splits):
    T, D = x2.shape
    n_in = w_bf16.shape[1]
    tm = TOKEN_TILE
    out_shape = [jax.ShapeDtypeStruct((T, D), F32)]
    out_specs = [pl.BlockSpec((tm, D), lambda i: (i, 0))]
    for width, _ in splits:
        out_shape.append(jax.ShapeDtypeStruct((T, width), BF16))
        out_specs.append(pl.BlockSpec((tm, width), lambda i: (i, 0)))
    return pl.pallas_call(
        functools.partial(_inproj_body, splits),
        grid=(T // tm,),
        in_specs=[
            pl.BlockSpec((tm, D), lambda i: (i, 0)),
            _const_spec((1, D)),
            _const_spec((1, D)),
            _const_spec((D, n_in)),
        ],
        out_specs=out_specs,
        out_shape=out_shape,
        compiler_params=_params(("parallel",)),
        name="inproj",
    )(x2, g, b, w_bf16)


def _alibi_slopes():
    return (2.0 ** (-8.0 * np.arange(1, N_HEADS + 1) / N_HEADS)).astype(np.float32)


def _attention_bias():
    slopes = _alibi_slopes().reshape(N_KV_HEADS, REP)
    i = np.arange(SUB)[None, :]
    j = np.arange(SUB)[:, None]
    out = np.empty((5, N_KV_HEADS, SUB, REP * SUB), np.float32)
    for var, (shift, masked) in enumerate([(-SUB, False), (-SUB, True), (0, False), (SUB, False), (SUB, True)]):
        dist = np.abs(i - (j + shift))
        valid = (dist <= WINDOW) & (not masked)
        for g in range(N_KV_HEADS):
            for r in range(REP):
                b = (-slopes[g, r] * dist.astype(np.float32)).astype(np.float64) * LOG2E
                out[var, g, :, r * SUB:(r + 1) * SUB] = np.where(valid, b, NEG_INF).astype(np.float32)
    return out


ONES_ROWS = 16


def _attention_body(n_sub, sink_ref, q_ref, kp_ref, kc_ref, kn_ref, vp_ref, vc_ref, vn_ref,
                    bias_ref, o_ref):
    n = pl.program_id(1)
    first = n == 0
    last = n == pl.num_programs(1) - 1
    kwin = jnp.concatenate([kp_ref[...], kc_ref[...], kn_ref[...]], axis=0)
    vwin = jnp.concatenate([vp_ref[...], vc_ref[...], vn_ref[...]], axis=0)
    vwin_t = vwin.astype(F32).T.astype(BF16)
    ones = jnp.ones((ONES_ROWS, vwin_t.shape[1]), BF16)

    def scores(j, g):
        qj = q_ref[j * SUB:(j + 1) * SUB, :]
        prev_var = jnp.where(first, 1, 0) if j == 0 else 0
        next_var = jnp.where(last, 4, 3) if j == n_sub - 1 else 3
        qg = jnp.concatenate(
            [qj[:, (g * REP + r) * HEAD_DIM:(g * REP + r + 1) * HEAD_DIM] for r in range(REP)], axis=0)
        kg = kwin[j * SUB:(j + 3) * SUB, g * HEAD_DIM:(g + 1) * HEAD_DIM]
        s = lax.dot_general(kg, qg, (((1,), (1,)), ((), ())), preferred_element_type=F32)
        bias = jnp.concatenate([bias_ref[prev_var, g], bias_ref[2, g], bias_ref[next_var, g]], axis=0)
        s = s + bias
        sink = jnp.concatenate(
            [jnp.full((1, SUB), sink_ref[g * REP + r], F32) for r in range(REP)], axis=1)
        m = jnp.maximum(jnp.max(s, axis=0, keepdims=True), sink)
        return s, m, sink

    def probs(s, m):
        return jnp.exp2(s - m).astype(BF16)

    def weighted_values(j, g, p, m, sink):
        vg_t = jnp.concatenate([vwin_t[g * HEAD_DIM:(g + 1) * HEAD_DIM], ones], axis=0)
        vg_t = vg_t[:, j * SUB:(j + 3) * SUB]
        ov = jnp.dot(vg_t, p, preferred_element_type=F32)
        denom = ov[HEAD_DIM:HEAD_DIM + 1] + jnp.exp2(sink - m)
        return ov[:HEAD_DIM] / denom

    units = [(j, g) for j in range(n_sub) for g in range(N_KV_HEADS)]
    stage1, stage2 = {}, {}
    heads_t = {j: [] for j in range(n_sub)}
    for step in range(len(units) + 2):
        if step >= 2:
            j, g = units[step - 2]
            p, m, sink = stage2.pop(step - 2)
            og_t = weighted_values(j, g, p, m, sink)
            heads_t[j] += [og_t[:, r * SUB:(r + 1) * SUB] for r in range(REP)]
            if g == N_KV_HEADS - 1:
                o_ref[j * SUB:(j + 1) * SUB, :] = jnp.concatenate(heads_t.pop(j), axis=0).T.astype(BF16)
        if 1 <= step <= len(units):
            s, m, sink = stage1.pop(step - 1)
            stage2[step - 1] = (probs(s, m), m, sink)
        if step < len(units):
            stage1[step] = scores(*units[step])


def _attention(q, k, v, sink, B, S):
    T, d_attn = q.shape
    d_kv = k.shape[1]
    tq = TOKEN_TILE
    n_sub = tq // SUB
    nq = S // tq
    nblk = S // SUB
    bias = jnp.asarray(_attention_bias())

    def q_map(b, n):
        return (b * nq + n, 0)

    def prev_map(b, n):
        return (b * nblk + jnp.maximum(n * n_sub - 1, 0), 0)

    def next_map(b, n):
        return (b * nblk + jnp.minimum((n + 1) * n_sub, nblk - 1), 0)

    kv_specs = [pl.BlockSpec((SUB, d_kv), prev_map), pl.BlockSpec((tq, d_kv), q_map),
                pl.BlockSpec((SUB, d_kv), next_map)]
    return pl.pallas_call(
        functools.partial(_attention_body, n_sub),
        grid=(B, nq),
        in_specs=[pl.BlockSpec(memory_space=pltpu.SMEM),
                  pl.BlockSpec((tq, d_attn), q_map)] + kv_specs + kv_specs + [_const_spec(bias.shape)],
        out_specs=pl.BlockSpec((tq, d_attn), q_map),
        out_shape=jax.ShapeDtypeStruct((T, d_attn), BF16),
        compiler_params=_params(("parallel", "parallel")),
        name="attention",
    )(sink, q, k, k, k, v, v, v, bias)


HALO = 16


def _mixer_out_body(tiles_per_seq, o_ref, cb_ref, cc_ref, cx_ref, ccp_ref, cxp_ref, ccn_ref, cxn_ref,
                    ga_ref, gc_ref, xn_ref, cw_ref, wa_ref, wc_ref, wo_ref, g_ref, b_ref, wr_ref,
                    x1_ref, x1b_ref, lg_ref):
    i = pl.program_id(0)
    n = i % tiles_per_seq
    tm = cc_ref.shape[0]
    u = cc_ref[...].astype(F32) * cx_ref[...].astype(F32)
    u_before = ccp_ref[HALO - 1:HALO, :].astype(F32) * cxp_ref[HALO - 1:HALO, :].astype(F32)
    u_after = ccn_ref[0:1, :].astype(F32) * cxn_ref[0:1, :].astype(F32)
    u_before = jnp.where(n == 0, 0.0, u_before)
    u_after = jnp.where(n == tiles_per_seq - 1, 0.0, u_after)
    row = lax.broadcasted_iota(jnp.int32, u.shape, 0)
    um1 = jnp.where(row == 0, u_before, pltpu.roll(u, 1, 0))
    up1 = jnp.where(row == tm - 1, u_after, pltpu.roll(u, tm - 1, 0))
    cw = cw_ref[...]
    y = cb_ref[...].astype(F32) * (cw[0:1, :] * um1 + cw[1:2, :] * u + cw[2:3, :] * up1)
    y_conv = jnp.dot(y.astype(BF16), wc_ref[...], preferred_element_type=F32)
    y_attn = jnp.dot(o_ref[...], wa_ref[...], preferred_element_type=F32)
    merged = (jax.nn.sigmoid(ga_ref[...].astype(F32)) * y_attn
              + jax.nn.sigmoid(gc_ref[...].astype(F32)) * y_conv)
    h = jnp.dot(merged.astype(BF16), wo_ref[...], preferred_element_type=F32)
    x1 = _layer_norm(DEEPNORM_ALPHA * xn_ref[...] + h, g_ref[...], b_ref[...])
    x1_ref[...] = x1
    x1b_ref[...] = x1.astype(BF16)
    lg_ref[...] = lax.dot_general(wr_ref[...], x1, (((1,), (1,)), ((), ())),
                                  precision=lax.Precision.HIGHEST, preferred_element_type=F32)


def _mixer_out(o, cb, cc, cx, ga, gc, xn, conv_w, wa, wc, wo, g, b, wr_t, S):
    T, D = xn.shape
    tm = TOKEN_TILE
    tiles_per_seq = S // tm
    hb = tm // HALO
    n_halo = T // HALO
    E = wr_t.shape[0]

    def tile(i):
        return (i, 0)

    def before(i):
        return (jnp.maximum(i * hb - 1, 0), 0)

    def after(i):
        return (jnp.minimum((i + 1) * hb, n_halo - 1), 0)

    tspec = pl.BlockSpec((tm, D), tile)
    return pl.pallas_call(
        functools.partial(_mixer_out_body, tiles_per_seq),
        grid=(T // tm,),
        in_specs=[tspec, tspec, tspec, tspec,
                  pl.BlockSpec((HALO, D), before), pl.BlockSpec((HALO, D), before),
                  pl.BlockSpec((HALO, D), after), pl.BlockSpec((HALO, D), after),
                  tspec, tspec, tspec,
                  _const_spec(conv_w.shape), _const_spec(wa.shape), _const_spec(wc.shape),
                  _const_spec(wo.shape), _const_spec(g.shape), _const_spec(b.shape),
                  _const_spec(wr_t.shape)],
        out_specs=[tspec, tspec, pl.BlockSpec((E, tm), lambda i: (0, i))],
        out_shape=[jax.ShapeDtypeStruct((T, D), F32), jax.ShapeDtypeStruct((T, D), BF16),
                   jax.ShapeDtypeStruct((E, T), F32)],
        compiler_params=_params(("parallel",)),
        name="mixer_out",
    )(o, cb, cc, cx, cc, cx, cc, cx, ga, gc, xn, conv_w, wa, wc, wo, g, b, wr_t)


SCAN = 256


def _exclusive_count(mask, tri):
    E, S = mask.shape
    mb = mask.astype(BF16)
    carry = jnp.zeros((E, 1), F32)
    parts = []
    for c in range(0, S, SCAN):
        chunk = mb[:, c:c + SCAN]
        parts.append(jnp.dot(chunk, tri, preferred_element_type=F32) + carry)
        carry = carry + jnp.sum(chunk.astype(F32), axis=1, keepdims=True)
    return jnp.concatenate(parts, axis=1)


def _route_body(cap, lg_ref, tri_ref, slot_ref, aff_ref):
    logits = lg_ref[...]
    mx = jnp.max(logits, axis=0, keepdims=True)
    ex = jnp.exp(logits - mx)
    aff = ex / jnp.sum(ex, axis=0, keepdims=True)
    aff_ref[...] = aff
    bits = pltpu.bitcast(aff, jnp.int32)
    E = bits.shape[0]

    def step(k, prefix):
        cand = prefix | lax.shift_left(jnp.int32(1), 30 - k)
        cnt = jnp.sum((bits >= cand).astype(F32), axis=1, keepdims=True)
        return jnp.where(cnt >= cap, cand, prefix)

    thresh = lax.fori_loop(0, 31, step, jnp.zeros((E, 1), jnp.int32))
    gt = bits > thresh
    eq = bits == thresh
    need = cap - jnp.sum(gt.astype(F32), axis=1, keepdims=True)
    tri = tri_ref[...]
    sel = gt | (eq & (_exclusive_count(eq, tri) < need))
    pos = _exclusive_count(sel, tri)
    slot_ref[...] = jnp.where(sel, pos, -1.0).astype(jnp.int32)


def _route(logits_t, B, S, cap):
    E = logits_t.shape[0]
    tri = jnp.asarray(np.triu(np.ones((SCAN, SCAN), np.float32), 1), BF16)
    spec = pl.BlockSpec((E, S), lambda b: (0, b))
    return pl.pallas_call(
        functools.partial(_route_body, cap),
        grid=(B,),
        in_specs=[spec, _const_spec(tri.shape)],
        out_specs=[spec, spec],
        out_shape=[jax.ShapeDtypeStruct((E, B * S), jnp.int32), jax.ShapeDtypeStruct((E, B * S), F32)],
        compiler_params=_params(("parallel",)),
        name="route",
    )(logits_t, tri)


def _experts_body(cap, slot_ref, aff_ref, x_ref, wg_ref, wu_ref, wd_ref, y_ref):
    slot = slot_ref[0]
    aff = aff_ref[0]
    S = slot.shape[1]
    hit = lax.broadcasted_iota(jnp.int32, (cap, S), 0) == slot
    onehot = jnp.where(hit, 1.0, 0.0).astype(BF16)
    xe = jnp.dot(onehot, x_ref[...], preferred_element_type=F32).astype(BF16)
    top_val = jnp.sum(jnp.where(hit, aff, 0.0), axis=1, keepdims=True)
    gate = jnp.dot(xe, wg_ref[0], preferred_element_type=F32)
    up = jnp.dot(xe, wu_ref[0], preferred_element_type=F32)
    h = (gate * jax.nn.sigmoid(gate) * up).astype(BF16)
    ye = jnp.dot(h, wd_ref[0], preferred_element_type=F32) * top_val
    y_ref[0, 0] = ye.astype(BF16)


def _experts(slot, aff, x1b, wg, wu, wd, B, S, cap):
    E, D, F = wg.shape
    slot3 = slot.reshape(E, 1, B * S)
    aff3 = aff.reshape(E, 1, B * S)
    row = pl.BlockSpec((1, 1, S), lambda e, b: (e, 0, b))
    return pl.pallas_call(
        functools.partial(_experts_body, cap),
        grid=(E, B),
        in_specs=[row, row,
                  pl.BlockSpec((S, D), lambda e, b: (b, 0)),
                  pl.BlockSpec((1, D, F), lambda e, b: (e, 0, 0), pipeline_mode=pl.Buffered(1)),
                  pl.BlockSpec((1, D, F), lambda e, b: (e, 0, 0), pipeline_mode=pl.Buffered(1)),
                  pl.BlockSpec((1, F, D), lambda e, b: (e, 0, 0), pipeline_mode=pl.Buffered(1))],
        out_specs=pl.BlockSpec((1, 1, cap, D), lambda e, b: (b, e, 0, 0)),
        out_shape=jax.ShapeDtypeStruct((B, E, cap, D), BF16),
        compiler_params=_params(("arbitrary", "arbitrary")),
        name="experts",
    )(slot3, aff3, x1b, wg, wu, wd)


def _combine_body(cap, slot_ref, y_ref, x1_ref, g_ref, b_ref, o_ref):
    E = slot_ref.shape[0]
    tm = slot_ref.shape[1]
    ci = lax.broadcasted_iota(jnp.int32, (cap, tm), 0)
    pieces = [jnp.where(ci == slot_ref[e:e + 1, :], 1.0, 0.0).astype(BF16) for e in range(E)]
    scatter = jnp.concatenate(pieces, axis=0)
    f = lax.dot_general(scatter, y_ref[0], (((0,), (0,)), ((), ())), preferred_element_type=F32)
    o_ref[...] = _layer_norm(DEEPNORM_ALPHA * x1_ref[...] + f, g_ref[...], b_ref[...])


def _combine(slot, y, x1, g, b, B, S, cap):
    T, D = x1.shape
    E = slot.shape[0]
    tm = TOKEN_TILE
    tiles_per_seq = S // tm
    y2 = y.reshape(B, E * cap, D)
    return pl.pallas_call(
        functools.partial(_combine_body, cap),
        grid=(T // tm,),
        in_specs=[pl.BlockSpec((E, tm), lambda i: (0, i)),
                  pl.BlockSpec((1, E * cap, D), lambda i: (i // tiles_per_seq, 0, 0)),
                  pl.BlockSpec((tm, D), lambda i: (i, 0)),
                  _const_spec(g.shape), _const_spec(b.shape)],
        out_specs=pl.BlockSpec((tm, D), lambda i: (i, 0)),
        out_shape=jax.ShapeDtypeStruct((T, D), F32),
        compiler_params=_params(("parallel",)),
        name="combine",
    )(slot, y2, x1, g, b)


def kernel(x, ln0_g, ln0_b, w_in, conv_w, attn_sink, w_attn_o, w_conv_o, w_out, ln1_g, ln1_b,
           w_router, w_gate, w_up, w_down, ln2_g, ln2_b):
    B, S, D = x.shape
    assert w_in.shape[0] == DEPTH == 1
    assert S % TOKEN_TILE == 0 and TOKEN_TILE % SUB == 0
    T = B * S
    d_attn = N_HEADS * HEAD_DIM
    d_kv = N_KV_HEADS * HEAD_DIM
    cap = CAPACITY_FACTOR * S // N_EXPERTS
    row = lambda a: a.reshape(1, -1).astype(F32)

    splits = ((d_attn, HEAD_DIM ** -0.5 * LOG2E), (d_kv, 1.0), (d_kv, 1.0),
              (D, 1.0), (D, 1.0), (D, 1.0), (D, 1.0), (D, 1.0))
    xn, q, k, v, cb, cc, cx, ga, gc = _inproj(
        x.reshape(T, D), row(ln0_g), row(ln0_b), w_in[0].astype(BF16), splits)

    o = _attention(q, k, v, attn_sink[0].astype(F32) * LOG2E, B, S)

    x1, x1b, logits_t = _mixer_out(
        o, cb, cc, cx, ga, gc, xn, conv_w[0].astype(F32),
        w_attn_o[0].astype(BF16), w_conv_o[0].astype(BF16), w_out[0].astype(BF16),
        row(ln1_g[0]), row(ln1_b[0]), w_router[0].T.astype(F32), S)

    slot, aff = _route(logits_t, B, S, cap)

    y = _experts(slot, aff, x1b, w_gate[0].astype(BF16), w_up[0].astype(BF16),
                 w_down[0].astype(BF16), B, S, cap)

    out = _combine(slot, y, x1, row(ln2_g[0]), row(ln2_b[0]), B, S, cap)
    return out.reshape(B, S, D)
```

```python
import functools

import jax
import jax.numpy as jnp
import numpy as np
from jax import lax
from jax.experimental import pallas as pl
from jax.experimental.pallas import tpu as pltpu

F32 = jnp.float32
BF16 = jnp.bfloat16

N_HEADS = 16
N_KV_HEADS = 4
HEAD_DIM = 64
REP = N_HEADS // N_KV_HEADS
WINDOW = 128
BLOCK = 128
NEG_INF = -1e30
LOG2E = float(np.log2(np.e))
CONV_WIDTH = 3
N_EXPERTS = 16
CAPACITY_FACTOR = 2
DEPTH = 1
DEEPNORM_ALPHA = (2.0 * DEPTH) ** 0.25
LN_EPS = 1e-5

V7X_VMEM_BYTES = 64 * 1024 * 1024
VMEM_LIMIT = 56 * 1024 * 1024

TOKEN_TILE = 512
SUB = BLOCK


def _layer_norm(x, g, b):
    mu = jnp.mean(x, axis=-1, keepdims=True)
    xc = x - mu
    var = jnp.mean(xc * xc, axis=-1, keepdims=True)
    return xc * lax.rsqrt(var + LN_EPS) * g + b


def _params(sem):
    return pltpu.CompilerParams(dimension_semantics=sem, vmem_limit_bytes=VMEM_LIMIT)


def _const_spec(shape):
    zeros = (0,) * len(shape)
    return pl.BlockSpec(shape, lambda *_: zeros, pipeline_mode=pl.Buffered(1))


def _inproj_body(splits, x_ref, g_ref, b_ref, w_ref, xn_ref, *out_refs):
    xn = _layer_norm(x_ref[...], g_ref[...], b_ref[...])
    xn_ref[...] = xn
    xb = xn.astype(BF16)
    off = 0
    for (width, scale), o_ref in zip(splits, out_refs):
        chunk = min(width, 512)
        for c in range(0, width, chunk):
            acc = jnp.dot(xb, w_ref[:, off + c:off + c + chunk], preferred_element_type=F32)
            if scale != 1.0:
                acc = acc * scale
            o_ref[:, c:c + chunk] = acc.astype(BF16)
        off += width


def _inproj(x2, g, b, w_bf16, splits):
    T, D = x2.shape
    n_in = w_bf16.shape[1]
    tm = TOKEN_TILE
    out_shape = [jax.ShapeDtypeStruct((T, D), F32)]
    out_specs = [pl.BlockSpec((tm, D), lambda i: (i, 0))]
    for width, _ in splits:
        out_shape.append(jax.ShapeDtypeStruct((T, width), BF16))
        out_specs.append(pl.BlockSpec((tm, width), lambda i: (i, 0)))
    return pl.pallas_call(
        functools.partial(_inproj_body, splits),
        grid=(T // tm,),
        in_specs=[
            pl.BlockSpec((tm, D), lambda i: (i, 0)),
            _const_spec((1, D)),
            _const_spec((1, D)),
            _const_spec((D, n_in)),
        ],
        out_specs=out_specs,
        out_shape=out_shape,
        compiler_params=_params(("parallel",)),
        name="inproj",
    )(x2, g, b, w_bf16)


def _alibi_slopes():
    return (2.0 ** (-8.0 * np.arange(1, N_HEADS + 1) / N_HEADS)).astype(np.float32)


def _attention_bias():
    slopes = _alibi_slopes().reshape(N_KV_HEADS, REP)
    i = np.arange(SUB)[None, :]
    j = np.arange(SUB)[:, None]
    out = np.empty((5, N_KV_HEADS, SUB, REP * SUB), np.float32)
    for var, (shift, masked) in enumerate([(-SUB, False), (-SUB, True), (0, False), (SUB, False), (SUB, True)]):
        dist = np.abs(i - (j + shift))
        valid = (dist <= WINDOW) & (not masked)
        for g in range(N_KV_HEADS):
            for r in range(REP):
                b = (-slopes[g, r] * dist.astype(np.float32)).astype(np.float64) * LOG2E
                out[var, g, :, r * SUB:(r + 1) * SUB] = np.where(valid, b, NEG_INF).astype(np.float32)
    return out


ONES_ROWS = 16


def _attention_body(n_sub, sink_ref, q_ref, kp_ref, kc_ref, kn_ref, vp_ref, vc_ref, vn_ref,
                    bias_ref, o_ref):
    n = pl.program_id(1)
    first = n == 0
    last = n == pl.num_programs(1) - 1
    kwin = jnp.concatenate([kp_ref[...], kc_ref[...], kn_ref[...]], axis=0)
    vwin = jnp.concatenate([vp_ref[...], vc_ref[...], vn_ref[...]], axis=0)
    vwin_t = vwin.astype(F32).T.astype(BF16)
    ones = jnp.ones((ONES_ROWS, vwin_t.shape[1]), BF16)

    def scores(j, g):
        qj = q_ref[j * SUB:(j + 1) * SUB, :]
        prev_var = jnp.where(first, 1, 0) if j == 0 else 0
        next_var = jnp.where(last, 4, 3) if j == n_sub - 1 else 3
        qg = jnp.concatenate(
            [qj[:, (g * REP + r) * HEAD_DIM:(g * REP + r + 1) * HEAD_DIM] for r in range(REP)], axis=0)
        kg = kwin[j * SUB:(j + 3) * SUB, g * HEAD_DIM:(g + 1) * HEAD_DIM]
        s = lax.dot_general(kg, qg, (((1,), (1,)), ((), ())), preferred_element_type=F32)
        bias = jnp.concatenate([bias_ref[prev_var, g], bias_ref[2, g], bias_ref[next_var, g]], axis=0)
        s = s + bias
        sink = jnp.concatenate(
            [jnp.full((1, SUB), sink_ref[g * REP + r], F32) for r in range(REP)], axis=1)
        m = jnp.maximum(jnp.max(s, axis=0, keepdims=True), sink)
        return s, m, sink

    def probs(s, m):
        return jnp.exp2(s - m).astype(BF16)

    def weighted_values(j, g, p, m, sink):
        vg_t = jnp.concatenate([vwin_t[g * HEAD_DIM:(g + 1) * HEAD_DIM], ones], axis=0)
        vg_t = vg_t[:, j * SUB:(j + 3) * SUB]
        ov = jnp.dot(vg_t, p, preferred_element_type=F32)
        denom = ov[HEAD_DIM:HEAD_DIM + 1] + jnp.exp2(sink - m)
        return ov[:HEAD_DIM] / denom

    units = [(j, g) for j in range(n_sub) for g in range(N_KV_HEADS)]
    stage1, stage2 = {}, {}
    heads_t = {j: [] for j in range(n_sub)}
    for step in range(len(units) + 2):
        if step >= 2:
            j, g = units[step - 2]
            p, m, sink = stage2.pop(step - 2)
            og_t = weighted_values(j, g, p, m, sink)
            heads_t[j] += [og_t[:, r * SUB:(r + 1) * SUB] for r in range(REP)]
            if g == N_KV_HEADS - 1:
                o_ref[j * SUB:(j + 1) * SUB, :] = jnp.concatenate(heads_t.pop(j), axis=0).T.astype(BF16)
        if 1 <= step <= len(units):
            s, m, sink = stage1.pop(step - 1)
            stage2[step - 1] = (probs(s, m), m, sink)
        if step < len(units):
            stage1[step] = scores(*units[step])


def _attention(q, k, v, sink, B, S):
    T, d_attn = q.shape
    d_kv = k.shape[1]
    tq = TOKEN_TILE
    n_sub = tq // SUB
    nq = S // tq
    nblk = S // SUB
    bias = jnp.asarray(_attention_bias())

    def q_map(b, n):
        return (b * nq + n, 0)

    def prev_map(b, n):
        return (b * nblk + jnp.maximum(n * n_sub - 1, 0), 0)

    def next_map(b, n):
        return (b * nblk + jnp.minimum((n + 1) * n_sub, nblk - 1), 0)

    kv_specs = [pl.BlockSpec((SUB, d_kv), prev_map), pl.BlockSpec((tq, d_kv), q_map),
                pl.BlockSpec((SUB, d_kv), next_map)]
    return pl.pallas_call(
        functools.partial(_attention_body, n_sub),
        grid=(B, nq),
        in_specs=[pl.BlockSpec(memory_space=pltpu.SMEM),
                  pl.BlockSpec((tq, d_attn), q_map)] + kv_specs + kv_specs + [_const_spec(bias.shape)],
        out_specs=pl.BlockSpec((tq, d_attn), q_map),
        out_shape=jax.ShapeDtypeStruct((T, d_attn), BF16),
        compiler_params=_params(("parallel", "parallel")),
        name="attention",
    )(sink, q, k, k, k, v, v, v, bias)


HALO = 16
MIXER_PARTS = 2


def _mixer_out_body(tiles_per_seq, o_ref, cb_ref, cc_ref, cx_ref, ccp_ref, cxp_ref, ccn_ref, cxn_ref,
                    ga_ref, gc_ref, xn_ref, cw_ref, wa_ref, wc_ref, wo_ref, g_ref, b_ref, wr_ref,
                    x1_ref, x1b_ref, lg_ref):
    i = pl.program_id(0)
    n = i % tiles_per_seq
    tm = cc_ref.shape[0]
    pm = tm // MIXER_PARTS
    u = cc_ref[...].astype(F32) * cx_ref[...].astype(F32)
    u_before = ccp_ref[HALO - 1:HALO, :].astype(F32) * cxp_ref[HALO - 1:HALO, :].astype(F32)
    u_after = ccn_ref[0:1, :].astype(F32) * cxn_ref[0:1, :].astype(F32)
    u_before = jnp.where(n == 0, 0.0, u_before)
    u_after = jnp.where(n == tiles_per_seq - 1, 0.0, u_after)
    cw = cw_ref[...]
    row = lax.broadcasted_iota(jnp.int32, (pm, u.shape[1]), 0)

    def project_in(p):
        r = pl.ds(p * pm, pm)
        y_attn = jnp.dot(o_ref[r, :], wa_ref[...], preferred_element_type=F32)
        up = u[p * pm:(p + 1) * pm]
        before = u_before if p == 0 else u[p * pm - 1:p * pm]
        after = u_after if p == MIXER_PARTS - 1 else u[(p + 1) * pm:(p + 1) * pm + 1]
        um1 = jnp.where(row == 0, before, pltpu.roll(up, 1, 0))
        up1 = jnp.where(row == pm - 1, after, pltpu.roll(up, pm - 1, 0))
        y = cb_ref[r, :].astype(F32) * (cw[0:1, :] * um1 + cw[1:2, :] * up + cw[2:3, :] * up1)
        y_conv = jnp.dot(y.astype(BF16), wc_ref[...], preferred_element_type=F32)
        return y_attn, y_conv

    def project_out(p, y_attn, y_conv):
        r = pl.ds(p * pm, pm)
        merged = (jax.nn.sigmoid(ga_ref[r, :].astype(F32)) * y_attn
                  + jax.nn.sigmoid(gc_ref[r, :].astype(F32)) * y_conv)
        return jnp.dot(merged.astype(BF16), wo_ref[...], preferred_element_type=F32)

    def finish(p, h):
        r = pl.ds(p * pm, pm)
        x1 = _layer_norm(DEEPNORM_ALPHA * xn_ref[r, :] + h, g_ref[...], b_ref[...])
        x1_ref[r, :] = x1
        x1_hi = x1.astype(BF16)
        x1b_ref[r, :] = x1_hi
        x1_lo = (x1 - x1_hi.astype(F32)).astype(BF16)
        nt = (((1,), (1,)), ((), ()))
        E = lg_ref.shape[0]
        by_hi = lax.dot_general(wr_ref[...], x1_hi, nt, preferred_element_type=F32)
        by_lo = lax.dot_general(wr_ref[0:E, :], x1_lo, nt, preferred_element_type=F32)
        lg_ref[:, r] = by_hi[0:E] + (by_hi[E:2 * E] + by_lo)

    stage1, stage2 = {}, {}
    for step in range(MIXER_PARTS + 2):
        if step < MIXER_PARTS:
            stage1[step] = project_in(step)
        if 0 <= step - 1 < MIXER_PARTS:
            stage2[step - 1] = project_out(step - 1, *stage1.pop(step - 1))
        if 0 <= step - 2 < MIXER_PARTS:
            finish(step - 2, stage2.pop(step - 2))


def _mixer_out(o, cb, cc, cx, ga, gc, xn, conv_w, wa, wc, wo, g, b, wr_t, S):
    T, D = xn.shape
    tm = TOKEN_TILE
    tiles_per_seq = S // tm
    hb = tm // HALO
    n_halo = T // HALO
    E = wr_t.shape[0] // 2

    def tile(i):
        return (i, 0)

    def before(i):
        return (jnp.maximum(i * hb - 1, 0), 0)

    def after(i):
        return (jnp.minimum((i + 1) * hb, n_halo - 1), 0)

    tspec = pl.BlockSpec((tm, D), tile)
    return pl.pallas_call(
        functools.partial(_mixer_out_body, tiles_per_seq),
        grid=(T // tm,),
        in_specs=[tspec, tspec, tspec, tspec,
                  pl.BlockSpec((HALO, D), before), pl.BlockSpec((HALO, D), before),
                  pl.BlockSpec((HALO, D), after), pl.BlockSpec((HALO, D), after),
                  tspec, tspec, tspec,
                  _const_spec(conv_w.shape), _const_spec(wa.shape), _const_spec(wc.shape),
                  _const_spec(wo.shape), _const_spec(g.shape), _const_spec(b.shape),
                  _const_spec(wr_t.shape)],
        out_specs=[tspec, tspec, pl.BlockSpec((E, tm), lambda i: (0, i))],
        out_shape=[jax.ShapeDtypeStruct((T, D), F32), jax.ShapeDtypeStruct((T, D), BF16),
                   jax.ShapeDtypeStruct((E, T), F32)],
        compiler_params=_params(("parallel",)),
        name="mixer_out",
    )(o, cb, cc, cx, cc, cx, cc, cx, ga, gc, xn, conv_w, wa, wc, wo, g, b, wr_t)


SCAN = 256


def _exclusive_count(mask, tri):
    E, S = mask.shape
    mb = mask.astype(BF16)
    carry = jnp.zeros((E, 1), F32)
    parts = []
    for c in range(0, S, SCAN):
        chunk = mb[:, c:c + SCAN]
        parts.append(jnp.dot(chunk, tri, preferred_element_type=F32) + carry)
        carry = carry + jnp.sum(chunk.astype(F32), axis=1, keepdims=True)
    return jnp.concatenate(parts, axis=1)


def _route_body(cap, lg_ref, tri_ref, slot_ref, aff_ref):
    logits = lg_ref[...]
    mx = jnp.max(logits, axis=0, keepdims=True)
    ex = jnp.exp(logits - mx)
    aff = ex / jnp.sum(ex, axis=0, keepdims=True)
    aff_ref[...] = aff
    bits = pltpu.bitcast(aff, jnp.int32)
    E = bits.shape[0]

    def step(k, prefix):
        cand = prefix | lax.shift_left(jnp.int32(1), 30 - k)
        cnt = jnp.sum((bits >= cand).astype(F32), axis=1, keepdims=True)
        return jnp.where(cnt >= cap, cand, prefix)

    thresh = lax.fori_loop(0, 31, step, jnp.zeros((E, 1), jnp.int32))
    gt = bits > thresh
    eq = bits == thresh
    need = cap - jnp.sum(gt.astype(F32), axis=1, keepdims=True)
    tri = tri_ref[...]
    sel = gt | (eq & (_exclusive_count(eq, tri) < need))
    pos = _exclusive_count(sel, tri)
    slot_ref[...] = jnp.where(sel, pos, -1.0).astype(jnp.int32)


def _route(logits_t, B, S, cap):
    E = logits_t.shape[0]
    tri = jnp.asarray(np.triu(np.ones((SCAN, SCAN), np.float32), 1), BF16)
    spec = pl.BlockSpec((E, S), lambda b: (0, b))
    return pl.pallas_call(
        functools.partial(_route_body, cap),
        grid=(B,),
        in_specs=[spec, _const_spec(tri.shape)],
        out_specs=[spec, spec],
        out_shape=[jax.ShapeDtypeStruct((E, B * S), jnp.int32), jax.ShapeDtypeStruct((E, B * S), F32)],
        compiler_params=_params(("parallel",)),
        name="route",
    )(logits_t, tri)


def _experts_body(cap, slot_ref, aff_ref, x_ref, wg_ref, wu_ref, wd_ref, y_ref):
    slot = slot_ref[0]
    aff = aff_ref[0]
    S = slot.shape[1]
    hit = lax.broadcasted_iota(jnp.int32, (cap, S), 0) == slot
    onehot = jnp.where(hit, 1.0, 0.0).astype(BF16)
    xe = jnp.dot(onehot, x_ref[...], preferred_element_type=F32).astype(BF16)
    top_val = jnp.sum(jnp.where(hit, aff, 0.0), axis=1, keepdims=True)
    gate = jnp.dot(xe, wg_ref[0], preferred_element_type=F32)
    up = jnp.dot(xe, wu_ref[0], preferred_element_type=F32)
    h = (gate * jax.nn.sigmoid(gate) * up).astype(BF16)
    ye = jnp.dot(h, wd_ref[0], preferred_element_type=F32) * top_val
    y_ref[0, 0] = ye.astype(BF16)


def _experts(slot, aff, x1b, wg, wu, wd, B, S, cap):
    E, D, F = wg.shape
    slot3 = slot.reshape(E, 1, B * S)
    aff3 = aff.reshape(E, 1, B * S)
    row = pl.BlockSpec((1, 1, S), lambda e, b: (e, 0, b))
    return pl.pallas_call(
        functools.partial(_experts_body, cap),
        grid=(E, B),
        in_specs=[row, row,
                  pl.BlockSpec((S, D), lambda e, b: (b, 0)),
                  pl.BlockSpec((1, D, F), lambda e, b: (e, 0, 0), pipeline_mode=pl.Buffered(1)),
                  pl.BlockSpec((1, D, F), lambda e, b: (e, 0, 0), pipeline_mode=pl.Buffered(1)),
                  pl.BlockSpec((1, F, D), lambda e, b: (e, 0, 0), pipeline_mode=pl.Buffered(1))],
        out_specs=pl.BlockSpec((1, 1, cap, D), lambda e, b: (b, e, 0, 0)),
        out_shape=jax.ShapeDtypeStruct((B, E, cap, D), BF16),
        compiler_params=_params(("arbitrary", "arbitrary")),
        name="experts",
    )(slot3, aff3, x1b, wg, wu, wd)


def _combine_body(cap, slot_ref, y_ref, x1_ref, g_ref, b_ref, o_ref):
    E = slot_ref.shape[0]
    tm = slot_ref.shape[1]
    ci = lax.broadcasted_iota(jnp.int32, (cap, tm), 0)
    pieces = [jnp.where(ci == slot_ref[e:e + 1, :], 1.0, 0.0).astype(BF16) for e in range(E)]
    scatter = jnp.concatenate(pieces, axis=0)
    f = lax.dot_general(scatter, y_ref[0], (((0,), (0,)), ((), ())), preferred_element_type=F32)
    o_ref[...] = _layer_norm(DEEPNORM_ALPHA * x1_ref[...] + f, g_ref[...], b_ref[...])


def _combine(slot, y, x1, g, b, B, S, cap):
    T, D = x1.shape
    E = slot.shape[0]
    tm = TOKEN_TILE
    tiles_per_seq = S // tm
    y2 = y.reshape(B, E * cap, D)
    return pl.pallas_call(
        functools.partial(_combine_body, cap),
        grid=(T // tm,),
        in_specs=[pl.BlockSpec((E, tm), lambda i: (0, i)),
                  pl.BlockSpec((1, E * cap, D), lambda i: (i // tiles_per_seq, 0, 0)),
                  pl.BlockSpec((tm, D), lambda i: (i, 0)),
                  _const_spec(g.shape), _const_spec(b.shape)],
        out_specs=pl.BlockSpec((tm, D), lambda i: (i, 0)),
        out_shape=jax.ShapeDtypeStruct((T, D), F32),
        compiler_params=_params(("parallel",)),
        name="combine",
    )(slot, y2, x1, g, b)


def _split_hi_lo(w):
    hi = w.astype(BF16)
    lo = (w - hi.astype(F32)).astype(BF16)
    return jnp.concatenate([hi, lo], axis=0)


def kernel(x, ln0_g, ln0_b, w_in, conv_w, attn_sink, w_attn_o, w_conv_o, w_out, ln1_g, ln1_b,
           w_router, w_gate, w_up, w_down, ln2_g, ln2_b):
    B, S, D = x.shape
    assert w_in.shape[0] == DEPTH == 1
    assert S % TOKEN_TILE == 0 and TOKEN_TILE % SUB == 0
    T = B * S
    d_attn = N_HEADS * HEAD_DIM
    d_kv = N_KV_HEADS * HEAD_DIM
    cap = CAPACITY_FACTOR * S // N_EXPERTS
    row = lambda a: a.reshape(1, -1).astype(F32)

    splits = ((d_attn, HEAD_DIM ** -0.5 * LOG2E), (d_kv, 1.0), (d_kv, 1.0),
              (D, 1.0), (D, 1.0), (D, 1.0), (D, 1.0), (D, 1.0))
    xn, q, k, v, cb, cc, cx, ga, gc = _inproj(
        x.reshape(T, D), row(ln0_g), row(ln0_b), w_in[0].astype(BF16), splits)

    o = _attention(q, k, v, attn_sink[0].astype(F32) * LOG2E, B, S)

    x1, x1b, logits_t = _mixer_out(
        o, cb, cc, cx, ga, gc, xn, conv_w[0].astype(F32),
        w_attn_o[0].astype(BF16), w_conv_o[0].astype(BF16), w_out[0].astype(BF16),
        row(ln1_g[0]), row(ln1_b[0]), _split_hi_lo(w_router[0].T.astype(F32)), S)

    slot, aff = _route(logits_t, B, S, cap)

    y = _experts(slot, aff, x1b, w_gate[0].astype(BF16), w_up[0].astype(BF16),
                 w_down[0].astype(BF16), B, S, cap)

    out = _combine(slot, y, x1, row(ln2_g[0]), row(ln2_b[0]), B, S, cap)
    return out.reshape(B, S, D)
```

```python
import functools

import jax
import jax.numpy as jnp
import numpy as np
from jax import lax
from jax.experimental import pallas as pl
from jax.experimental.pallas import tpu as pltpu

F32 = jnp.float32
BF16 = jnp.bfloat16

N_HEADS = 16
N_KV_HEADS = 4
HEAD_DIM = 64
REP = N_HEADS // N_KV_HEADS
WINDOW = 128
BLOCK = 128
NEG_INF = -1e30
LOG2E = float(np.log2(np.e))
CONV_WIDTH = 3
N_EXPERTS = 16
CAPACITY_FACTOR = 2
DEPTH = 1
DEEPNORM_ALPHA = (2.0 * DEPTH) ** 0.25
LN_EPS = 1e-5

V7X_VMEM_BYTES = 64 * 1024 * 1024
LANES = 128
VMEM_LIMIT = 56 * 1024 * 1024

TOKEN_TILE = 512
SUB = BLOCK


def _layer_norm(x, g, b):
    mu = jnp.mean(x, axis=-1, keepdims=True)
    xc = x - mu
    var = jnp.mean(xc * xc, axis=-1, keepdims=True)
    return xc * lax.rsqrt(var + LN_EPS) * g + b


def _params(sem):
    return pltpu.CompilerParams(dimension_semantics=sem, vmem_limit_bytes=VMEM_LIMIT)


def _const_spec(shape):
    zeros = (0,) * len(shape)
    return pl.BlockSpec(shape, lambda *_: zeros, pipeline_mode=pl.Buffered(1))


def _inproj_body(splits, x_ref, g_ref, b_ref, w_ref, xn_ref, *out_refs):
    xn = _layer_norm(x_ref[...], g_ref[...], b_ref[...])
    xn_ref[...] = xn
    xb = xn.astype(BF16)
    off = 0
    for (width, scale), o_ref in zip(splits, out_refs):
        chunk = min(width, 512)
        for c in range(0, width, chunk):
            acc = jnp.dot(xb, w_ref[:, off + c:off + c + chunk], preferred_element_type=F32)
            if scale != 1.0:
                acc = acc * scale
            o_ref[:, c:c + chunk] = acc.astype(BF16)
        off += width


def _inproj(x2, g, b, w_bf16, splits):
    T, D = x2.shape
    n_in = w_bf16.shape[1]
    tm = TOKEN_TILE
    out_shape = [jax.ShapeDtypeStruct((T, D), F32)]
    out_specs = [pl.BlockSpec((tm, D), lambda i: (i, 0))]
    for width, _ in splits:
        out_shape.append(jax.ShapeDtypeStruct((T, width), BF16))
        out_specs.append(pl.BlockSpec((tm, width), lambda i: (i, 0)))
    return pl.pallas_call(
        functools.partial(_inproj_body, splits),
        grid=(T // tm,),
        in_specs=[
            pl.BlockSpec((tm, D), lambda i: (i, 0)),
            _const_spec((1, D)),
            _const_spec((1, D)),
            _const_spec((D, n_in)),
        ],
        out_specs=out_specs,
        out_shape=out_shape,
        compiler_params=_params(("parallel",)),
        name="inproj",
    )(x2, g, b, w_bf16)


def _alibi_slopes():
    return (2.0 ** (-8.0 * np.arange(1, N_HEADS + 1) / N_HEADS)).astype(np.float32)


def _attention_bias():
    slopes = _alibi_slopes().reshape(N_KV_HEADS, REP)
    i = np.arange(SUB)[None, :]
    j = np.arange(SUB)[:, None]
    out = np.empty((5, N_KV_HEADS, SUB, REP * SUB), np.float32)
    for var, (shift, masked) in enumerate([(-SUB, False), (-SUB, True), (0, False), (SUB, False), (SUB, True)]):
        dist = np.abs(i - (j + shift))
        valid = (dist <= WINDOW) & (not masked)
        for g in range(N_KV_HEADS):
            for r in range(REP):
                b = (-slopes[g, r] * dist.astype(np.float32)).astype(np.float64) * LOG2E
                out[var, g, :, r * SUB:(r + 1) * SUB] = np.where(valid, b, NEG_INF).astype(np.float32)
    return out


ONES_ROWS = 16


def _attention_body(n_sub, sink_ref, q_ref, kp_ref, kc_ref, kn_ref, vp_ref, vc_ref, vn_ref,
                    bias_ref, o_ref):
    n = pl.program_id(1)
    first = n == 0
    last = n == pl.num_programs(1) - 1
    kwin = jnp.concatenate([kp_ref[...], kc_ref[...], kn_ref[...]], axis=0)
    vwin = jnp.concatenate([vp_ref[...], vc_ref[...], vn_ref[...]], axis=0)
    vwin_t = vwin.astype(F32).T.astype(BF16)
    ones = jnp.ones((ONES_ROWS, vwin_t.shape[1]), BF16)

    def scores(j, g):
        qj = q_ref[j * SUB:(j + 1) * SUB, :]
        prev_var = jnp.where(first, 1, 0) if j == 0 else 0
        next_var = jnp.where(last, 4, 3) if j == n_sub - 1 else 3
        qg = jnp.concatenate(
            [qj[:, (g * REP + r) * HEAD_DIM:(g * REP + r + 1) * HEAD_DIM] for r in range(REP)], axis=0)
        kg = kwin[j * SUB:(j + 3) * SUB, g * HEAD_DIM:(g + 1) * HEAD_DIM]
        s = lax.dot_general(kg, qg, (((1,), (1,)), ((), ())), preferred_element_type=F32)
        bias = jnp.concatenate([bias_ref[prev_var, g], bias_ref[2, g], bias_ref[next_var, g]], axis=0)
        s = s + bias
        sink = jnp.concatenate(
            [jnp.full((1, SUB), sink_ref[g * REP + r], F32) for r in range(REP)], axis=1)
        m = jnp.maximum(jnp.max(s, axis=0, keepdims=True), sink)
        return s, m, sink

    def probs(s, m):
        return jnp.exp2(s - m).astype(BF16)

    def weighted_values(j, g, p, m, sink):
        vg_t = jnp.concatenate([vwin_t[g * HEAD_DIM:(g + 1) * HEAD_DIM], ones], axis=0)
        vg_t = vg_t[:, j * SUB:(j + 3) * SUB]
        ov = jnp.dot(vg_t, p, preferred_element_type=F32)
        denom = ov[HEAD_DIM:HEAD_DIM + 1] + jnp.exp2(sink - m)
        return ov[:HEAD_DIM] / denom

    units = [(j, g) for j in range(n_sub) for g in range(N_KV_HEADS)]
    stage1, stage2 = {}, {}
    heads_t = {j: [] for j in range(n_sub)}
    for step in range(len(units) + 2):
        if step >= 2:
            j, g = units[step - 2]
            p, m, sink = stage2.pop(step - 2)
            og_t = weighted_values(j, g, p, m, sink)
            heads_t[j] += [og_t[:, r * SUB:(r + 1) * SUB] for r in range(REP)]
            if g == N_KV_HEADS - 1:
                o_ref[j * SUB:(j + 1) * SUB, :] = jnp.concatenate(heads_t.pop(j), axis=0).T.astype(BF16)
        if 1 <= step <= len(units):
            s, m, sink = stage1.pop(step - 1)
            stage2[step - 1] = (probs(s, m), m, sink)
        if step < len(units):
            stage1[step] = scores(*units[step])


def _attention(q, k, v, sink, B, S):
    T, d_attn = q.shape
    d_kv = k.shape[1]
    tq = TOKEN_TILE
    n_sub = tq // SUB
    nq = S // tq
    nblk = S // SUB
    bias = jnp.asarray(_attention_bias())

    def q_map(b, n):
        return (b * nq + n, 0)

    def prev_map(b, n):
        return (b * nblk + jnp.maximum(n * n_sub - 1, 0), 0)

    def next_map(b, n):
        return (b * nblk + jnp.minimum((n + 1) * n_sub, nblk - 1), 0)

    kv_specs = [pl.BlockSpec((SUB, d_kv), prev_map), pl.BlockSpec((tq, d_kv), q_map),
                pl.BlockSpec((SUB, d_kv), next_map)]
    return pl.pallas_call(
        functools.partial(_attention_body, n_sub),
        grid=(B, nq),
        in_specs=[pl.BlockSpec(memory_space=pltpu.SMEM),
                  pl.BlockSpec((tq, d_attn), q_map)] + kv_specs + kv_specs + [_const_spec(bias.shape)],
        out_specs=pl.BlockSpec((tq, d_attn), q_map),
        out_shape=jax.ShapeDtypeStruct((T, d_attn), BF16),
        compiler_params=_params(("parallel", "parallel")),
        name="attention",
    )(sink, q, k, k, k, v, v, v, bias)


HALO = 16
MIXER_PARTS = 2


def _mixer_out_body(tiles_per_seq, o_ref, cb_ref, cc_ref, cx_ref, ccp_ref, cxp_ref, ccn_ref, cxn_ref,
                    ga_ref, gc_ref, xn_ref, cw_ref, wa_ref, wc_ref, wo_ref, g_ref, b_ref, wr_ref,
                    x1_ref, x1b_ref, lg_ref):
    i = pl.program_id(0)
    n = i % tiles_per_seq
    tm = cc_ref.shape[0]
    pm = tm // MIXER_PARTS
    u = cc_ref[...].astype(F32) * cx_ref[...].astype(F32)
    u_before = ccp_ref[HALO - 1:HALO, :].astype(F32) * cxp_ref[HALO - 1:HALO, :].astype(F32)
    u_after = ccn_ref[0:1, :].astype(F32) * cxn_ref[0:1, :].astype(F32)
    u_before = jnp.where(n == 0, 0.0, u_before)
    u_after = jnp.where(n == tiles_per_seq - 1, 0.0, u_after)
    cw = cw_ref[...]
    row = lax.broadcasted_iota(jnp.int32, (pm, u.shape[1]), 0)

    def project_in(p):
        r = pl.ds(p * pm, pm)
        y_attn = jnp.dot(o_ref[r, :], wa_ref[...], preferred_element_type=F32)
        up = u[p * pm:(p + 1) * pm]
        before = u_before if p == 0 else u[p * pm - 1:p * pm]
        after = u_after if p == MIXER_PARTS - 1 else u[(p + 1) * pm:(p + 1) * pm + 1]
        um1 = jnp.where(row == 0, before, pltpu.roll(up, 1, 0))
        up1 = jnp.where(row == pm - 1, after, pltpu.roll(up, pm - 1, 0))
        y = cb_ref[r, :].astype(F32) * (cw[0:1, :] * um1 + cw[1:2, :] * up + cw[2:3, :] * up1)
        y_conv = jnp.dot(y.astype(BF16), wc_ref[...], preferred_element_type=F32)
        return y_attn, y_conv

    def project_out(p, y_attn, y_conv):
        r = pl.ds(p * pm, pm)
        merged = (jax.nn.sigmoid(ga_ref[r, :].astype(F32)) * y_attn
                  + jax.nn.sigmoid(gc_ref[r, :].astype(F32)) * y_conv)
        return jnp.dot(merged.astype(BF16), wo_ref[...], preferred_element_type=F32)

    def finish(p, h):
        r = pl.ds(p * pm, pm)
        x1 = _layer_norm(DEEPNORM_ALPHA * xn_ref[r, :] + h, g_ref[...], b_ref[...])
        x1_ref[r, :] = x1
        x1_hi = x1.astype(BF16)
        x1b_ref[r, :] = x1_hi
        x1_lo = (x1 - x1_hi.astype(F32)).astype(BF16)
        nt = (((1,), (1,)), ((), ()))
        E = lg_ref.shape[0]
        by_hi = lax.dot_general(wr_ref[...], x1_hi, nt, preferred_element_type=F32)
        by_lo = lax.dot_general(wr_ref[0:E, :], x1_lo, nt, preferred_element_type=F32)
        lg_ref[:, r] = by_hi[0:E] + (by_hi[E:2 * E] + by_lo)

    stage1, stage2 = {}, {}
    for step in range(MIXER_PARTS + 2):
        if step < MIXER_PARTS:
            stage1[step] = project_in(step)
        if 0 <= step - 1 < MIXER_PARTS:
            stage2[step - 1] = project_out(step - 1, *stage1.pop(step - 1))
        if 0 <= step - 2 < MIXER_PARTS:
            finish(step - 2, stage2.pop(step - 2))


def _mixer_out(o, cb, cc, cx, ga, gc, xn, conv_w, wa, wc, wo, g, b, wr_t, S):
    T, D = xn.shape
    tm = TOKEN_TILE
    tiles_per_seq = S // tm
    hb = tm // HALO
    n_halo = T // HALO
    E = wr_t.shape[0] // 2

    def tile(i):
        return (i, 0)

    def before(i):
        return (jnp.maximum(i * hb - 1, 0), 0)

    def after(i):
        return (jnp.minimum((i + 1) * hb, n_halo - 1), 0)

    tspec = pl.BlockSpec((tm, D), tile)
    return pl.pallas_call(
        functools.partial(_mixer_out_body, tiles_per_seq),
        grid=(T // tm,),
        in_specs=[tspec, tspec, tspec, tspec,
                  pl.BlockSpec((HALO, D), before), pl.BlockSpec((HALO, D), before),
                  pl.BlockSpec((HALO, D), after), pl.BlockSpec((HALO, D), after),
                  tspec, tspec, tspec,
                  _const_spec(conv_w.shape), _const_spec(wa.shape), _const_spec(wc.shape),
                  _const_spec(wo.shape), _const_spec(g.shape), _const_spec(b.shape),
                  _const_spec(wr_t.shape)],
        out_specs=[tspec, tspec, pl.BlockSpec((E, tm), lambda i: (0, i))],
        out_shape=[jax.ShapeDtypeStruct((T, D), F32), jax.ShapeDtypeStruct((T, D), BF16),
                   jax.ShapeDtypeStruct((E, T), F32)],
        compiler_params=_params(("parallel",)),
        name="mixer_out",
    )(o, cb, cc, cx, cc, cx, cc, cx, ga, gc, xn, conv_w, wa, wc, wo, g, b, wr_t)


SCAN = 256
CHUNK = 256
WINDOW_ROWS = 96


def _exclusive_count(mask, tri):
    E, S = mask.shape
    mb = mask.astype(BF16)
    carry = jnp.zeros((E, 1), F32)
    parts = []
    for c in range(0, S, SCAN):
        chunk = mb[:, c:c + SCAN]
        parts.append(jnp.dot(chunk, tri, preferred_element_type=F32) + carry)
        carry = carry + jnp.sum(chunk.astype(F32), axis=1, keepdims=True)
    return jnp.concatenate(parts, axis=1)


def _route_body(cap, lg_ref, tri_ref, slot_ref, aff_ref, start_ref):
    logits = lg_ref[...]
    mx = jnp.max(logits, axis=0, keepdims=True)
    ex = jnp.exp(logits - mx)
    aff = ex / jnp.sum(ex, axis=0, keepdims=True)
    aff_ref[...] = aff
    bits = pltpu.bitcast(aff, jnp.int32)
    E = bits.shape[0]

    def step(k, prefix):
        cand = prefix | lax.shift_left(jnp.int32(1), 30 - k)
        cnt = jnp.sum((bits >= cand).astype(F32), axis=1, keepdims=True)
        return jnp.where(cnt >= cap, cand, prefix)

    thresh = lax.fori_loop(0, 31, step, jnp.zeros((E, 1), jnp.int32))
    gt = bits > thresh
    eq = bits == thresh
    need = cap - jnp.sum(gt.astype(F32), axis=1, keepdims=True)
    tri = tri_ref[...]
    sel = gt | (eq & (_exclusive_count(eq, tri) < need))
    pos = _exclusive_count(sel, tri)
    slot_ref[...] = jnp.where(sel, pos, -1.0).astype(jnp.int32)
    S = pos.shape[1]
    start_ref[0] = jnp.concatenate([pos[:, c:c + 1] for c in range(0, S, CHUNK)], axis=1).astype(jnp.int32)


def _route(logits_t, B, S, cap):
    E = logits_t.shape[0]
    tri = jnp.asarray(np.triu(np.ones((SCAN, SCAN), np.float32), 1), BF16)
    spec = pl.BlockSpec((E, S), lambda b: (0, b))
    return pl.pallas_call(
        functools.partial(_route_body, cap),
        grid=(B,),
        in_specs=[spec, _const_spec(tri.shape)],
        out_specs=[spec, spec, pl.BlockSpec((1, E, S // CHUNK), lambda b: (b, 0, 0))],
        out_shape=[jax.ShapeDtypeStruct((E, B * S), jnp.int32), jax.ShapeDtypeStruct((E, B * S), F32),
                   jax.ShapeDtypeStruct((B, E, S // CHUNK), jnp.int32)],
        compiler_params=_params(("parallel",)),
        name="route",
    )(logits_t, tri)


def _window(starts_ref, first, kt, n_chunks, cap):
    start = starts_ref[first + kt]
    nxt = starts_ref[first + jnp.minimum(kt + 1, n_chunks - 1)]
    end = jnp.where(kt + 1 < n_chunks, nxt, cap)
    base = jnp.minimum(lax.shift_left(lax.shift_right_logical(start, 4), 4), cap - WINDOW_ROWS)
    return pl.multiple_of(base, 16), end <= base + WINDOW_ROWS


def _experts_body(cap, starts_ref, slot_ref, aff_ref, x_ref, wg_ref, wu_ref, wd_ref, y_ref,
                  xe_ref, tv_ref):
    e = pl.program_id(0)
    b = pl.program_id(1)
    n_exp = pl.num_programs(0)
    slot = slot_ref[0]
    aff = aff_ref[0]
    S = slot.shape[1]
    n_chunks = S // CHUNK
    first = (b * n_exp + e) * n_chunks
    windows = [_window(starts_ref, first, kt, n_chunks, cap) for kt in range(n_chunks)]
    fits = functools.reduce(jnp.logical_and, [w[1] for w in windows])

    @pl.when(fits)
    def _():
        xe_ref[...] = jnp.zeros_like(xe_ref)
        tv_ref[...] = jnp.zeros_like(tv_ref)
        for kt in range(n_chunks):
            cols = slice(kt * CHUNK, (kt + 1) * CHUNK)
            base = windows[kt][0]
            rows = pl.ds(base, WINDOW_ROWS)
            hit = (lax.broadcasted_iota(jnp.int32, (WINDOW_ROWS, CHUNK), 0) + base) == slot[:, cols]
            onehot = jnp.where(hit, 1.0, 0.0).astype(BF16)
            xe_ref[rows, :] += jnp.dot(onehot, x_ref[cols, :], preferred_element_type=F32)
            tv = jnp.sum(jnp.where(hit, aff[:, cols], 0.0), axis=1, keepdims=True)
            tv_ref[rows, :] += jnp.broadcast_to(tv, (WINDOW_ROWS, tv_ref.shape[1]))

    @pl.when(jnp.logical_not(fits))
    def _():
        hit = lax.broadcasted_iota(jnp.int32, (cap, S), 0) == slot
        onehot = jnp.where(hit, 1.0, 0.0).astype(BF16)
        xe_ref[...] = jnp.dot(onehot, x_ref[...], preferred_element_type=F32)
        tv = jnp.sum(jnp.where(hit, aff, 0.0), axis=1, keepdims=True)
        tv_ref[...] = jnp.broadcast_to(tv, tv_ref.shape)

    xe = xe_ref[...].astype(BF16)
    top_val = tv_ref[:, 0:1]
    gate = jnp.dot(xe, wg_ref[0], preferred_element_type=F32)
    up = jnp.dot(xe, wu_ref[0], preferred_element_type=F32)
    h = (gate * jax.nn.sigmoid(gate) * up).astype(BF16)
    ye = jnp.dot(h, wd_ref[0], preferred_element_type=F32) * top_val
    y_ref[0, 0] = ye.astype(BF16)


def _experts(starts, slot, aff, x1b, wg, wu, wd, B, S, cap):
    E, D, F = wg.shape
    slot3 = slot.reshape(E, 1, B * S)
    aff3 = aff.reshape(E, 1, B * S)
    row = pl.BlockSpec((1, 1, S), lambda e, b, st: (e, 0, b))
    wspec = lambda shape: pl.BlockSpec(shape, lambda e, b, st: (e, 0, 0), pipeline_mode=pl.Buffered(1))
    return pl.pallas_call(
        functools.partial(_experts_body, cap),
        grid_spec=pltpu.PrefetchScalarGridSpec(
            num_scalar_prefetch=1,
            grid=(E, B),
            in_specs=[row, row, pl.BlockSpec((S, D), lambda e, b, st: (b, 0)),
                      wspec((1, D, F)), wspec((1, D, F)), wspec((1, F, D))],
            out_specs=pl.BlockSpec((1, 1, cap, D), lambda e, b, st: (b, e, 0, 0)),
            scratch_shapes=[pltpu.VMEM((cap, D), F32), pltpu.VMEM((cap, LANES), F32)]),
        out_shape=jax.ShapeDtypeStruct((B, E, cap, D), BF16),
        compiler_params=_params(("arbitrary", "arbitrary")),
        name="experts",
    )(starts, slot3, aff3, x1b, wg, wu, wd)


def _combine_body(cap, tiles_per_seq, starts_ref, slot_ref, y_ref, x1_ref, g_ref, b_ref, o_ref, f_ref):
    i = pl.program_id(0)
    E = slot_ref.shape[0]
    tm = slot_ref.shape[1]
    subs = tm // CHUNK
    n_chunks = tiles_per_seq * subs
    b = i // tiles_per_seq
    nt = (((0,), (0,)), ((), ()))
    for sub in range(subs):
        kt = (i % tiles_per_seq) * subs + sub
        cols = slice(sub * CHUNK, (sub + 1) * CHUNK)
        windows = [_window(starts_ref, (b * E + e) * n_chunks, kt, n_chunks, cap) for e in range(E)]
        fits = functools.reduce(jnp.logical_and, [w[1] for w in windows])

        @pl.when(fits)
        def _():
            ri = lax.broadcasted_iota(jnp.int32, (WINDOW_ROWS, CHUNK), 0)
            onehot = jnp.concatenate(
                [jnp.where(ri + windows[e][0] == slot_ref[e:e + 1, cols], 1.0, 0.0).astype(BF16)
                 for e in range(E)], axis=0)
            ywin = jnp.concatenate(
                [y_ref[0, e, pl.ds(windows[e][0], WINDOW_ROWS), :] for e in range(E)], axis=0)
            f_ref[cols, :] = lax.dot_general(onehot, ywin, nt, preferred_element_type=F32)

        @pl.when(jnp.logical_not(fits))
        def _():
            ri = lax.broadcasted_iota(jnp.int32, (cap, CHUNK), 0)
            onehot = jnp.concatenate(
                [jnp.where(ri == slot_ref[e:e + 1, cols], 1.0, 0.0).astype(BF16) for e in range(E)], axis=0)
            yall = jnp.concatenate([y_ref[0, e] for e in range(E)], axis=0)
            f_ref[cols, :] = lax.dot_general(onehot, yall, nt, preferred_element_type=F32)

    o_ref[...] = _layer_norm(DEEPNORM_ALPHA * x1_ref[...] + f_ref[...], g_ref[...], b_ref[...])


def _combine(starts, slot, y, x1, g, b, B, S, cap):
    T, D = x1.shape
    E = slot.shape[0]
    tm = TOKEN_TILE
    tiles_per_seq = S // tm
    const = lambda shape: pl.BlockSpec(shape, lambda i, st: (0,) * len(shape), pipeline_mode=pl.Buffered(1))
    return pl.pallas_call(
        functools.partial(_combine_body, cap, tiles_per_seq),
        grid_spec=pltpu.PrefetchScalarGridSpec(
            num_scalar_prefetch=1,
            grid=(T // tm,),
            in_specs=[pl.BlockSpec((E, tm), lambda i, st: (0, i)),
                      pl.BlockSpec((1, E, cap, D), lambda i, st: (i // tiles_per_seq, 0, 0, 0)),
                      pl.BlockSpec((tm, D), lambda i, st: (i, 0)),
                      const(g.shape), const(b.shape)],
            out_specs=pl.BlockSpec((tm, D), lambda i, st: (i, 0)),
            scratch_shapes=[pltpu.VMEM((tm, D), F32)]),
        out_shape=jax.ShapeDtypeStruct((T, D), F32),
        compiler_params=_params(("arbitrary",)),
        name="combine",
    )(starts, slot, y, x1, g, b)


def _split_hi_lo(w):
    hi = w.astype(BF16)
    lo = (w - hi.astype(F32)).astype(BF16)
    return jnp.concatenate([hi, lo], axis=0)


def kernel(x, ln0_g, ln0_b, w_in, conv_w, attn_sink, w_attn_o, w_conv_o, w_out, ln1_g, ln1_b,
           w_router, w_gate, w_up, w_down, ln2_g, ln2_b):
    B, S, D = x.shape
    assert w_in.shape[0] == DEPTH == 1
    assert S % TOKEN_TILE == 0 and TOKEN_TILE % SUB == 0 and TOKEN_TILE % CHUNK == 0
    T = B * S
    d_attn = N_HEADS * HEAD_DIM
    d_kv = N_KV_HEADS * HEAD_DIM
    cap = CAPACITY_FACTOR * S // N_EXPERTS
    row = lambda a: a.reshape(1, -1).astype(F32)

    splits = ((d_attn, HEAD_DIM ** -0.5 * LOG2E), (d_kv, 1.0), (d_kv, 1.0),
              (D, 1.0), (D, 1.0), (D, 1.0), (D, 1.0), (D, 1.0))
    xn, q, k, v, cb, cc, cx, ga, gc = _inproj(
        x.reshape(T, D), row(ln0_g), row(ln0_b), w_in[0].astype(BF16), splits)

    o = _attention(q, k, v, attn_sink[0].astype(F32) * LOG2E, B, S)

    x1, x1b, logits_t = _mixer_out(
        o, cb, cc, cx, ga, gc, xn, conv_w[0].astype(F32),
        w_attn_o[0].astype(BF16), w_conv_o[0].astype(BF16), w_out[0].astype(BF16),
        row(ln1_g[0]), row(ln1_b[0]), _split_hi_lo(w_router[0].T.astype(F32)), S)

    slot, aff, starts = _route(logits_t, B, S, cap)
    starts = starts.reshape(-1)

    y = _experts(starts, slot, aff, x1b, w_gate[0].astype(BF16), w_up[0].astype(BF16),
                 w_down[0].astype(BF16), B, S, cap)

    out = _combine(starts, slot, y, x1, row(ln2_g[0]), row(ln2_b[0]), B, S, cap)
    return out.reshape(B, S, D)
```

```python
import functools

import jax
import jax.numpy as jnp
import numpy as np
from jax import lax
from jax.experimental import pallas as pl
from jax.experimental.pallas import tpu as pltpu

F32 = jnp.float32
BF16 = jnp.bfloat16

N_HEADS = 16
N_KV_HEADS = 4
HEAD_DIM = 64
REP = N_HEADS // N_KV_HEADS
WINDOW = 128
BLOCK = 128
NEG_INF = -1e30
LOG2E = float(np.log2(np.e))
CONV_WIDTH = 3
N_EXPERTS = 16
CAPACITY_FACTOR = 2
DEPTH = 1
DEEPNORM_ALPHA = (2.0 * DEPTH) ** 0.25
LN_EPS = 1e-5

V7X_VMEM_BYTES = 64 * 1024 * 1024
LANES = 128
VMEM_LIMIT = 56 * 1024 * 1024

TOKEN_TILE = 512
SUB = BLOCK


def _layer_norm(x, g, b):
    mu = jnp.mean(x, axis=-1, keepdims=True)
    xc = x - mu
    var = jnp.mean(xc * xc, axis=-1, keepdims=True)
    return xc * lax.rsqrt(var + LN_EPS) * g + b


def _params(sem):
    return pltpu.CompilerParams(dimension_semantics=sem, vmem_limit_bytes=VMEM_LIMIT)


def _const_spec(shape):
    zeros = (0,) * len(shape)
    return pl.BlockSpec(shape, lambda *_: zeros, pipeline_mode=pl.Buffered(1))


def _inproj_body(splits, x_ref, g_ref, b_ref, w_ref, xn_ref, *out_refs):
    xn = _layer_norm(x_ref[...], g_ref[...], b_ref[...])
    xn_ref[...] = xn
    xb = xn.astype(BF16)
    off = 0
    for (width, scale), o_ref in zip(splits, out_refs):
        chunk = min(width, 512)
        for c in range(0, width, chunk):
            acc = jnp.dot(xb, w_ref[:, off + c:off + c + chunk], preferred_element_type=F32)
            if scale != 1.0:
                acc = acc * scale
            o_ref[:, c:c + chunk] = acc.astype(BF16)
        off += width


def _inproj(x2, g, b, w_bf16, splits):
    T, D = x2.shape
    n_in = w_bf16.shape[1]
    tm = TOKEN_TILE
    out_shape = [jax.ShapeDtypeStruct((T, D), F32)]
    out_specs = [pl.BlockSpec((tm, D), lambda i: (i, 0))]
    for width, _ in splits:
        out_shape.append(jax.ShapeDtypeStruct((T, width), BF16))
        out_specs.append(pl.BlockSpec((tm, width), lambda i: (i, 0)))
    return pl.pallas_call(
        functools.partial(_inproj_body, splits),
        grid=(T // tm,),
        in_specs=[
            pl.BlockSpec((tm, D), lambda i: (i, 0)),
            _const_spec((1, D)),
            _const_spec((1, D)),
            _const_spec((D, n_in)),
        ],
        out_specs=out_specs,
        out_shape=out_shape,
        compiler_params=_params(("parallel",)),
        name="inproj",
    )(x2, g, b, w_bf16)


def _alibi_slopes():
    return (2.0 ** (-8.0 * np.arange(1, N_HEADS + 1) / N_HEADS)).astype(np.float32)


def _attention_bias():
    slopes = _alibi_slopes().reshape(N_KV_HEADS, REP)
    i = np.arange(SUB)[None, :]
    j = np.arange(SUB)[:, None]
    out = np.empty((5, N_KV_HEADS, SUB, REP * SUB), np.float32)
    for var, (shift, masked) in enumerate([(-SUB, False), (-SUB, True), (0, False), (SUB, False), (SUB, True)]):
        dist = np.abs(i - (j + shift))
        valid = (dist <= WINDOW) & (not masked)
        for g in range(N_KV_HEADS):
            for r in range(REP):
                b = (-slopes[g, r] * dist.astype(np.float32)).astype(np.float64) * LOG2E
                out[var, g, :, r * SUB:(r + 1) * SUB] = np.where(valid, b, NEG_INF).astype(np.float32)
    return out


ONES_ROWS = 16


def _attention_body(n_sub, sink_ref, q_ref, kp_ref, kc_ref, kn_ref, vp_ref, vc_ref, vn_ref,
                    bias_ref, o_ref):
    n = pl.program_id(1)
    first = n == 0
    last = n == pl.num_programs(1) - 1
    kwin = jnp.concatenate([kp_ref[...], kc_ref[...], kn_ref[...]], axis=0)
    vwin = jnp.concatenate([vp_ref[...], vc_ref[...], vn_ref[...]], axis=0)
    vwin_t = vwin.astype(F32).T.astype(BF16)
    ones = jnp.ones((ONES_ROWS, vwin_t.shape[1]), BF16)

    def scores(j, g):
        qj = q_ref[j * SUB:(j + 1) * SUB, :]
        prev_var = jnp.where(first, 1, 0) if j == 0 else 0
        next_var = jnp.where(last, 4, 3) if j == n_sub - 1 else 3
        qg = jnp.concatenate(
            [qj[:, (g * REP + r) * HEAD_DIM:(g * REP + r + 1) * HEAD_DIM] for r in range(REP)], axis=0)
        kg = kwin[j * SUB:(j + 3) * SUB, g * HEAD_DIM:(g + 1) * HEAD_DIM]
        s = lax.dot_general(kg, qg, (((1,), (1,)), ((), ())), preferred_element_type=F32)
        bias = jnp.concatenate([bias_ref[prev_var, g], bias_ref[2, g], bias_ref[next_var, g]], axis=0)
        s = s + bias
        sink = jnp.concatenate(
            [jnp.full((1, SUB), sink_ref[g * REP + r], F32) for r in range(REP)], axis=1)
        m = jnp.maximum(jnp.max(s, axis=0, keepdims=True), sink)
        return s, m, sink

    def probs(s, m):
        return jnp.exp2(s - m).astype(BF16)

    def weighted_values(j, g, p, m, sink):
        vg_t = jnp.concatenate([vwin_t[g * HEAD_DIM:(g + 1) * HEAD_DIM], ones], axis=0)
        vg_t = vg_t[:, j * SUB:(j + 3) * SUB]
        ov = jnp.dot(vg_t, p, preferred_element_type=F32)
        denom = ov[HEAD_DIM:HEAD_DIM + 1] + jnp.exp2(sink - m)
        return ov[:HEAD_DIM] / denom

    units = [(j, g) for j in range(n_sub) for g in range(N_KV_HEADS)]
    stage1, stage2 = {}, {}
    heads_t = {j: [] for j in range(n_sub)}
    for step in range(len(units) + 2):
        if step >= 2:
            j, g = units[step - 2]
            p, m, sink = stage2.pop(step - 2)
            og_t = weighted_values(j, g, p, m, sink)
            heads_t[j] += [og_t[:, r * SUB:(r + 1) * SUB] for r in range(REP)]
            if g == N_KV_HEADS - 1:
                o_ref[j * SUB:(j + 1) * SUB, :] = jnp.concatenate(heads_t.pop(j), axis=0).T.astype(BF16)
        if 1 <= step <= len(units):
            s, m, sink = stage1.pop(step - 1)
            stage2[step - 1] = (probs(s, m), m, sink)
        if step < len(units):
            stage1[step] = scores(*units[step])


def _attention(q, k, v, sink, B, S):
    T, d_attn = q.shape
    d_kv = k.shape[1]
    tq = TOKEN_TILE
    n_sub = tq // SUB
    nq = S // tq
    nblk = S // SUB
    bias = jnp.asarray(_attention_bias())

    def q_map(b, n):
        return (b * nq + n, 0)

    def prev_map(b, n):
        return (b * nblk + jnp.maximum(n * n_sub - 1, 0), 0)

    def next_map(b, n):
        return (b * nblk + jnp.minimum((n + 1) * n_sub, nblk - 1), 0)

    kv_specs = [pl.BlockSpec((SUB, d_kv), prev_map), pl.BlockSpec((tq, d_kv), q_map),
                pl.BlockSpec((SUB, d_kv), next_map)]
    return pl.pallas_call(
        functools.partial(_attention_body, n_sub),
        grid=(B, nq),
        in_specs=[pl.BlockSpec(memory_space=pltpu.SMEM),
                  pl.BlockSpec((tq, d_attn), q_map)] + kv_specs + kv_specs + [_const_spec(bias.shape)],
        out_specs=pl.BlockSpec((tq, d_attn), q_map),
        out_shape=jax.ShapeDtypeStruct((T, d_attn), BF16),
        compiler_params=_params(("parallel", "parallel")),
        name="attention",
    )(sink, q, k, k, k, v, v, v, bias)


HALO = 16
MIXER_PARTS = 2


def _mixer_out_body(tiles_per_seq, o_ref, cb_ref, cc_ref, cx_ref, ccp_ref, cxp_ref, ccn_ref, cxn_ref,
                    ga_ref, gc_ref, xn_ref, cw_ref, wa_ref, wc_ref, wo_ref, g_ref, b_ref, wr_ref,
                    x1_ref, x1b_ref, lg_ref):
    i = pl.program_id(0)
    n = i % tiles_per_seq
    tm = cc_ref.shape[0]
    pm = tm // MIXER_PARTS
    u = cc_ref[...].astype(F32) * cx_ref[...].astype(F32)
    u_before = ccp_ref[HALO - 1:HALO, :].astype(F32) * cxp_ref[HALO - 1:HALO, :].astype(F32)
    u_after = ccn_ref[0:1, :].astype(F32) * cxn_ref[0:1, :].astype(F32)
    u_before = jnp.where(n == 0, 0.0, u_before)
    u_after = jnp.where(n == tiles_per_seq - 1, 0.0, u_after)
    cw = cw_ref[...]
    row = lax.broadcasted_iota(jnp.int32, (pm, u.shape[1]), 0)

    def project_in(p):
        r = pl.ds(p * pm, pm)
        y_attn = jnp.dot(o_ref[r, :], wa_ref[...], preferred_element_type=F32)
        up = u[p * pm:(p + 1) * pm]
        before = u_before if p == 0 else u[p * pm - 1:p * pm]
        after = u_after if p == MIXER_PARTS - 1 else u[(p + 1) * pm:(p + 1) * pm + 1]
        um1 = jnp.where(row == 0, before, pltpu.roll(up, 1, 0))
        up1 = jnp.where(row == pm - 1, after, pltpu.roll(up, pm - 1, 0))
        y = cb_ref[r, :].astype(F32) * (cw[0:1, :] * um1 + cw[1:2, :] * up + cw[2:3, :] * up1)
        y_conv = jnp.dot(y.astype(BF16), wc_ref[...], preferred_element_type=F32)
        return y_attn, y_conv

    def project_out(p, y_attn, y_conv):
        r = pl.ds(p * pm, pm)
        merged = (jax.nn.sigmoid(ga_ref[r, :].astype(F32)) * y_attn
                  + jax.nn.sigmoid(gc_ref[r, :].astype(F32)) * y_conv)
        return jnp.dot(merged.astype(BF16), wo_ref[...], preferred_element_type=F32)

    def finish(p, h):
        r = pl.ds(p * pm, pm)
        x1 = _layer_norm(DEEPNORM_ALPHA * xn_ref[r, :] + h, g_ref[...], b_ref[...])
        x1_ref[r, :] = x1
        x1_hi = x1.astype(BF16)
        x1b_ref[r, :] = x1_hi
        x1_lo = (x1 - x1_hi.astype(F32)).astype(BF16)
        nt = (((1,), (1,)), ((), ()))
        E = lg_ref.shape[0]
        by_hi = lax.dot_general(wr_ref[...], x1_hi, nt, preferred_element_type=F32)
        by_lo = lax.dot_general(wr_ref[0:E, :], x1_lo, nt, preferred_element_type=F32)
        lg_ref[:, r] = by_hi[0:E] + (by_hi[E:2 * E] + by_lo)

    stage1, stage2 = {}, {}
    for step in range(MIXER_PARTS + 2):
        if step < MIXER_PARTS:
            stage1[step] = project_in(step)
        if 0 <= step - 1 < MIXER_PARTS:
            stage2[step - 1] = project_out(step - 1, *stage1.pop(step - 1))
        if 0 <= step - 2 < MIXER_PARTS:
            finish(step - 2, stage2.pop(step - 2))


def _mixer_out(o, cb, cc, cx, ga, gc, xn, conv_w, wa, wc, wo, g, b, wr_t, S):
    T, D = xn.shape
    tm = TOKEN_TILE
    tiles_per_seq = S // tm
    hb = tm // HALO
    n_halo = T // HALO
    E = wr_t.shape[0] // 2

    def tile(i):
        return (i, 0)

    def before(i):
        return (jnp.maximum(i * hb - 1, 0), 0)

    def after(i):
        return (jnp.minimum((i + 1) * hb, n_halo - 1), 0)

    tspec = pl.BlockSpec((tm, D), tile)
    return pl.pallas_call(
        functools.partial(_mixer_out_body, tiles_per_seq),
        grid=(T // tm,),
        in_specs=[tspec, tspec, tspec, tspec,
                  pl.BlockSpec((HALO, D), before), pl.BlockSpec((HALO, D), before),
                  pl.BlockSpec((HALO, D), after), pl.BlockSpec((HALO, D), after),
                  tspec, tspec, tspec,
                  _const_spec(conv_w.shape), _const_spec(wa.shape), _const_spec(wc.shape),
                  _const_spec(wo.shape), _const_spec(g.shape), _const_spec(b.shape),
                  _const_spec(wr_t.shape)],
        out_specs=[tspec, tspec, pl.BlockSpec((E, tm), lambda i: (0, i))],
        out_shape=[jax.ShapeDtypeStruct((T, D), F32), jax.ShapeDtypeStruct((T, D), BF16),
                   jax.ShapeDtypeStruct((E, T), F32)],
        compiler_params=_params(("parallel",)),
        name="mixer_out",
    )(o, cb, cc, cx, cc, cx, cc, cx, ga, gc, xn, conv_w, wa, wc, wo, g, b, wr_t)


VALUE_BITS = 32
RADIX_BITS = 3
SCAN = 256
CHUNK = 256
WINDOW_ROWS = 80


def _exclusive_count(mask, tri):
    E, S = mask.shape
    mb = mask.astype(BF16)
    carry = jnp.zeros((E, 1), F32)
    parts = []
    for c in range(0, S, SCAN):
        chunk = mb[:, c:c + SCAN]
        parts.append(jnp.dot(chunk, tri, preferred_element_type=F32) + carry)
        carry = carry + jnp.sum(chunk.astype(F32), axis=1, keepdims=True)
    return jnp.concatenate(parts, axis=1)


def _route_body(cap, lg_ref, tri_ref, slot_ref, aff_ref, start_ref):
    logits = lg_ref[...]
    mx = jnp.max(logits, axis=0, keepdims=True)
    ex = jnp.exp(logits - mx)
    aff = ex / jnp.sum(ex, axis=0, keepdims=True)
    aff_ref[...] = aff
    bits = pltpu.bitcast(aff, jnp.int32)
    E = bits.shape[0]

    def count_ge(cand):
        return jnp.sum((bits >= cand).astype(F32), axis=1, keepdims=True)

    def step(k, prefix):
        shift = VALUE_BITS - 1 - RADIX_BITS * (k + 1)
        out = prefix
        for digit in range(1, 2 ** RADIX_BITS):
            cand = prefix | lax.shift_left(jnp.int32(digit), shift)
            out = jnp.where(count_ge(cand) >= cap, cand, out)
        return out

    top = jnp.full((E, 1), 1 << (VALUE_BITS - 2), jnp.int32)
    prefix = jnp.where(count_ge(top) >= cap, top, 0)
    thresh = lax.fori_loop(0, (VALUE_BITS - 2) // RADIX_BITS, step, prefix)
    gt = bits > thresh
    eq = bits == thresh
    need = cap - jnp.sum(gt.astype(F32), axis=1, keepdims=True)
    tri = tri_ref[...]
    sel = gt | (eq & (_exclusive_count(eq, tri) < need))
    pos = _exclusive_count(sel, tri)
    slot_ref[...] = jnp.where(sel, pos, -1.0).astype(jnp.int32)
    S = pos.shape[1]
    start_ref[0] = jnp.concatenate([pos[:, c:c + 1] for c in range(0, S, CHUNK)], axis=1).astype(jnp.int32)


def _route(logits_t, B, S, cap):
    E = logits_t.shape[0]
    tri = jnp.asarray(np.triu(np.ones((SCAN, SCAN), np.float32), 1), BF16)
    spec = pl.BlockSpec((E, S), lambda b: (0, b))
    return pl.pallas_call(
        functools.partial(_route_body, cap),
        grid=(B,),
        in_specs=[spec, _const_spec(tri.shape)],
        out_specs=[spec, spec, pl.BlockSpec((1, E, S // CHUNK), lambda b: (b, 0, 0))],
        out_shape=[jax.ShapeDtypeStruct((E, B * S), jnp.int32), jax.ShapeDtypeStruct((E, B * S), F32),
                   jax.ShapeDtypeStruct((B, E, S // CHUNK), jnp.int32)],
        compiler_params=_params(("parallel",)),
        name="route",
    )(logits_t, tri)


def _window(starts_ref, first, kt, n_chunks, cap):
    start = starts_ref[first + kt]
    nxt = starts_ref[first + jnp.minimum(kt + 1, n_chunks - 1)]
    end = jnp.where(kt + 1 < n_chunks, nxt, cap)
    base = jnp.minimum(lax.shift_left(lax.shift_right_logical(start, 4), 4), cap - WINDOW_ROWS)
    return pl.multiple_of(base, 16), end <= base + WINDOW_ROWS


def _dispatch_body(cap, starts_ref, slot_ref, aff_ref, x_ref, xe_ref, tv_ref):
    b = pl.program_id(0)
    E, S = slot_ref.shape
    n_chunks = S // CHUNK
    windows = [[_window(starts_ref, (b * E + e) * n_chunks, kt, n_chunks, cap) for e in range(E)]
               for kt in range(n_chunks)]
    fits = functools.reduce(jnp.logical_and, [w[1] for per_chunk in windows for w in per_chunk])

    @pl.when(fits)
    def _():
        xe_ref[...] = jnp.zeros_like(xe_ref)
        tv_ref[...] = jnp.zeros_like(tv_ref)
        ri = lax.broadcasted_iota(jnp.int32, (WINDOW_ROWS, CHUNK), 0)
        for kt in range(n_chunks):
            cols = slice(kt * CHUNK, (kt + 1) * CHUNK)
            hits = [(ri + windows[kt][e][0]) == slot_ref[e:e + 1, cols] for e in range(E)]
            onehot = jnp.concatenate([jnp.where(h, 1.0, 0.0).astype(BF16) for h in hits], axis=0)
            rows_all = jnp.dot(onehot, x_ref[cols, :], preferred_element_type=F32)
            for e in range(E):
                rows = pl.ds(windows[kt][e][0], WINDOW_ROWS)
                xe_ref[0, e, rows, :] += rows_all[e * WINDOW_ROWS:(e + 1) * WINDOW_ROWS].astype(BF16)
                tv = jnp.sum(jnp.where(hits[e], aff_ref[e:e + 1, cols], 0.0), axis=1, keepdims=True)
                tv_ref[0, e, rows, :] += jnp.broadcast_to(tv, (WINDOW_ROWS, LANES))

    @pl.when(jnp.logical_not(fits))
    def _():
        ri = lax.broadcasted_iota(jnp.int32, (cap, S), 0)
        for e in range(E):
            hit = ri == slot_ref[e:e + 1, :]
            onehot = jnp.where(hit, 1.0, 0.0).astype(BF16)
            xe_ref[0, e] = jnp.dot(onehot, x_ref[...], preferred_element_type=F32).astype(BF16)
            tv = jnp.sum(jnp.where(hit, aff_ref[e:e + 1, :], 0.0), axis=1, keepdims=True)
            tv_ref[0, e] = jnp.broadcast_to(tv, (cap, LANES))


def _dispatch(starts, slot, aff, x1b, B, S, cap):
    E = slot.shape[0]
    D = x1b.shape[1]
    row = pl.BlockSpec((E, S), lambda b, st: (0, b))
    return pl.pallas_call(
        functools.partial(_dispatch_body, cap),
        grid_spec=pltpu.PrefetchScalarGridSpec(
            num_scalar_prefetch=1,
            grid=(B,),
            in_specs=[row, row, pl.BlockSpec((S, D), lambda b, st: (b, 0))],
            out_specs=[pl.BlockSpec((1, E, cap, D), lambda b, st: (b, 0, 0, 0)),
                       pl.BlockSpec((1, E, cap, LANES), lambda b, st: (b, 0, 0, 0))]),
        out_shape=[jax.ShapeDtypeStruct((B, E, cap, D), BF16),
                   jax.ShapeDtypeStruct((B, E, cap, LANES), F32)],
        compiler_params=_params(("arbitrary",)),
        name="dispatch",
    )(starts, slot, aff, x1b)


def _experts_body(xe_ref, tv_ref, wg_ref, wu_ref, wd_ref, y_ref):
    xe = xe_ref[0, 0]
    gate = jnp.dot(xe, wg_ref[0], preferred_element_type=F32)
    up = jnp.dot(xe, wu_ref[0], preferred_element_type=F32)
    h = (gate * jax.nn.sigmoid(gate) * up).astype(BF16)
    ye = jnp.dot(h, wd_ref[0], preferred_element_type=F32) * tv_ref[0, 0, :, 0:1]
    y_ref[0, 0] = ye.astype(BF16)


def _experts(xe, tv, wg, wu, wd):
    B, E, cap, D = xe.shape
    F = wg.shape[2]
    wspec = lambda shape: pl.BlockSpec(shape, lambda e, b: (e, 0, 0), pipeline_mode=pl.Buffered(1))
    tok = lambda width: pl.BlockSpec((1, 1, cap, width), lambda e, b: (b, e, 0, 0))
    return pl.pallas_call(
        _experts_body,
        grid=(E, B),
        in_specs=[tok(D), tok(LANES), wspec((1, D, F)), wspec((1, D, F)), wspec((1, F, D))],
        out_specs=tok(D),
        out_shape=jax.ShapeDtypeStruct((B, E, cap, D), BF16),
        compiler_params=_params(("arbitrary", "arbitrary")),
        name="experts",
    )(xe, tv, wg, wu, wd)


def _combine_body(cap, tiles_per_seq, starts_ref, slot_ref, y_ref, x1_ref, g_ref, b_ref, o_ref, f_ref):
    i = pl.program_id(0)
    E = slot_ref.shape[0]
    tm = slot_ref.shape[1]
    subs = tm // CHUNK
    n_chunks = tiles_per_seq * subs
    b = i // tiles_per_seq
    nt = (((0,), (0,)), ((), ()))
    for sub in range(subs):
        kt = (i % tiles_per_seq) * subs + sub
        cols = slice(sub * CHUNK, (sub + 1) * CHUNK)
        windows = [_window(starts_ref, (b * E + e) * n_chunks, kt, n_chunks, cap) for e in range(E)]
        fits = functools.reduce(jnp.logical_and, [w[1] for w in windows])

        @pl.when(fits)
        def _():
            ri = lax.broadcasted_iota(jnp.int32, (WINDOW_ROWS, CHUNK), 0)
            onehot = jnp.concatenate(
                [jnp.where(ri + windows[e][0] == slot_ref[e:e + 1, cols], 1.0, 0.0).astype(BF16)
                 for e in range(E)], axis=0)
            ywin = jnp.concatenate(
                [y_ref[0, e, pl.ds(windows[e][0], WINDOW_ROWS), :] for e in range(E)], axis=0)
            f_ref[cols, :] = lax.dot_general(onehot, ywin, nt, preferred_element_type=F32)

        @pl.when(jnp.logical_not(fits))
        def _():
            ri = lax.broadcasted_iota(jnp.int32, (cap, CHUNK), 0)
            onehot = jnp.concatenate(
                [jnp.where(ri == slot_ref[e:e + 1, cols], 1.0, 0.0).astype(BF16) for e in range(E)], axis=0)
            yall = jnp.concatenate([y_ref[0, e] for e in range(E)], axis=0)
            f_ref[cols, :] = lax.dot_general(onehot, yall, nt, preferred_element_type=F32)

    o_ref[...] = _layer_norm(DEEPNORM_ALPHA * x1_ref[...] + f_ref[...], g_ref[...], b_ref[...])


def _combine(starts, slot, y, x1, g, b, B, S, cap):
    T, D = x1.shape
    E = slot.shape[0]
    tm = TOKEN_TILE
    tiles_per_seq = S // tm
    const = lambda shape: pl.BlockSpec(shape, lambda i, st: (0,) * len(shape), pipeline_mode=pl.Buffered(1))
    return pl.pallas_call(
        functools.partial(_combine_body, cap, tiles_per_seq),
        grid_spec=pltpu.PrefetchScalarGridSpec(
            num_scalar_prefetch=1,
            grid=(T // tm,),
            in_specs=[pl.BlockSpec((E, tm), lambda i, st: (0, i)),
                      pl.BlockSpec((1, E, cap, D), lambda i, st: (i // tiles_per_seq, 0, 0, 0)),
                      pl.BlockSpec((tm, D), lambda i, st: (i, 0)),
                      const(g.shape), const(b.shape)],
            out_specs=pl.BlockSpec((tm, D), lambda i, st: (i, 0)),
            scratch_shapes=[pltpu.VMEM((tm, D), F32)]),
        out_shape=jax.ShapeDtypeStruct((T, D), F32),
        compiler_params=_params(("arbitrary",)),
        name="combine",
    )(starts, slot, y, x1, g, b)


def _split_hi_lo(w):
    hi = w.astype(BF16)
    lo = (w - hi.astype(F32)).astype(BF16)
    return jnp.concatenate([hi, lo], axis=0)


def kernel(x, ln0_g, ln0_b, w_in, conv_w, attn_sink, w_attn_o, w_conv_o, w_out, ln1_g, ln1_b,
           w_router, w_gate, w_up, w_down, ln2_g, ln2_b):
    B, S, D = x.shape
    assert w_in.shape[0] == DEPTH == 1
    assert S % TOKEN_TILE == 0 and TOKEN_TILE % SUB == 0 and TOKEN_TILE % CHUNK == 0
    T = B * S
    d_attn = N_HEADS * HEAD_DIM
    d_kv = N_KV_HEADS * HEAD_DIM
    cap = CAPACITY_FACTOR * S // N_EXPERTS
    row = lambda a: a.reshape(1, -1).astype(F32)

    splits = ((d_attn, HEAD_DIM ** -0.5 * LOG2E), (d_kv, 1.0), (d_kv, 1.0),
              (D, 1.0), (D, 1.0), (D, 1.0), (D, 1.0), (D, 1.0))
    xn, q, k, v, cb, cc, cx, ga, gc = _inproj(
        x.reshape(T, D), row(ln0_g), row(ln0_b), w_in[0].astype(BF16), splits)

    o = _attention(q, k, v, attn_sink[0].astype(F32) * LOG2E, B, S)

    x1, x1b, logits_t = _mixer_out(
        o, cb, cc, cx, ga, gc, xn, conv_w[0].astype(F32),
        w_attn_o[0].astype(BF16), w_conv_o[0].astype(BF16), w_out[0].astype(BF16),
        row(ln1_g[0]), row(ln1_b[0]), _split_hi_lo(w_router[0].T.astype(F32)), S)

    slot, aff, starts = _route(logits_t, B, S, cap)
    starts = starts.reshape(-1)

    xe, top_val = _dispatch(starts, slot, aff, x1b, B, S, cap)

    y = _experts(xe, top_val, w_gate[0].astype(BF16), w_up[0].astype(BF16), w_down[0].astype(BF16))

    out = _combine(starts, slot, y, x1, row(ln2_g[0]), row(ln2_b[0]), B, S, cap)
    return out.reshape(B, S, D)
```

```python
import functools

import jax
import jax.numpy as jnp
import numpy as np
from jax import lax
from jax.experimental import pallas as pl
from jax.experimental.pallas import tpu as pltpu

F32 = jnp.float32
BF16 = jnp.bfloat16

N_HEADS = 16
N_KV_HEADS = 4
HEAD_DIM = 64
REP = N_HEADS // N_KV_HEADS
WINDOW = 128
BLOCK = 128
NEG_INF = -1e30
LOG2E = float(np.log2(np.e))
CONV_WIDTH = 3
N_EXPERTS = 16
CAPACITY_FACTOR = 2
DEPTH = 1
DEEPNORM_ALPHA = (2.0 * DEPTH) ** 0.25
LN_EPS = 1e-5

V7X_VMEM_BYTES = 64 * 1024 * 1024
LANES = 128
VMEM_LIMIT = 56 * 1024 * 1024

TOKEN_TILE = 512
SUB = BLOCK


def _layer_norm(x, g, b):
    mu = jnp.mean(x, axis=-1, keepdims=True)
    xc = x - mu
    var = jnp.mean(xc * xc, axis=-1, keepdims=True)
    return xc * lax.rsqrt(var + LN_EPS) * g + b


def _params(sem):
    return pltpu.CompilerParams(dimension_semantics=sem, vmem_limit_bytes=VMEM_LIMIT)


def _const_spec(shape):
    zeros = (0,) * len(shape)
    return pl.BlockSpec(shape, lambda *_: zeros, pipeline_mode=pl.Buffered(1))


def _inproj_body(splits, x_ref, g_ref, b_ref, w_ref, xn_ref, *out_refs):
    xn = _layer_norm(x_ref[...], g_ref[...], b_ref[...])
    xn_ref[...] = xn
    xb = xn.astype(BF16)
    off = 0
    for (width, scale), o_ref in zip(splits, out_refs):
        chunk = min(width, 512)
        for c in range(0, width, chunk):
            acc = jnp.dot(xb, w_ref[:, off + c:off + c + chunk], preferred_element_type=F32)
            if scale != 1.0:
                acc = acc * scale
            o_ref[:, c:c + chunk] = acc.astype(BF16)
        off += width


def _inproj(x2, g, b, w_bf16, splits):
    T, D = x2.shape
    n_in = w_bf16.shape[1]
    tm = TOKEN_TILE
    out_shape = [jax.ShapeDtypeStruct((T, D), F32)]
    out_specs = [pl.BlockSpec((tm, D), lambda i: (i, 0))]
    for width, _ in splits:
        out_shape.append(jax.ShapeDtypeStruct((T, width), BF16))
        out_specs.append(pl.BlockSpec((tm, width), lambda i: (i, 0)))
    return pl.pallas_call(
        functools.partial(_inproj_body, splits),
        grid=(T // tm,),
        in_specs=[
            pl.BlockSpec((tm, D), lambda i: (i, 0)),
            _const_spec((1, D)),
            _const_spec((1, D)),
            _const_spec((D, n_in)),
        ],
        out_specs=out_specs,
        out_shape=out_shape,
        compiler_params=_params(("parallel",)),
        name="inproj",
    )(x2, g, b, w_bf16)


def _alibi_slopes():
    return (2.0 ** (-8.0 * np.arange(1, N_HEADS + 1) / N_HEADS)).astype(np.float32)


def _attention_bias():
    slopes = _alibi_slopes().reshape(N_KV_HEADS, REP)
    i = np.arange(SUB)[None, :]
    j = np.arange(SUB)[:, None]
    out = np.empty((5, N_KV_HEADS, SUB, REP * SUB), np.float32)
    for var, (shift, masked) in enumerate([(-SUB, False), (-SUB, True), (0, False), (SUB, False), (SUB, True)]):
        dist = np.abs(i - (j + shift))
        valid = (dist <= WINDOW) & (not masked)
        for g in range(N_KV_HEADS):
            for r in range(REP):
                b = (-slopes[g, r] * dist.astype(np.float32)).astype(np.float64) * LOG2E
                out[var, g, :, r * SUB:(r + 1) * SUB] = np.where(valid, b, NEG_INF).astype(np.float32)
    return out


ONES_ROWS = 16


def _attention_body(n_sub, sink_ref, q_ref, kp_ref, kc_ref, kn_ref, vp_ref, vc_ref, vn_ref,
                    bias_ref, o_ref):
    n = pl.program_id(1)
    first = n == 0
    last = n == pl.num_programs(1) - 1
    kwin = jnp.concatenate([kp_ref[...], kc_ref[...], kn_ref[...]], axis=0)
    vwin = jnp.concatenate([vp_ref[...], vc_ref[...], vn_ref[...]], axis=0)
    vwin_t = vwin.astype(F32).T.astype(BF16)
    ones = jnp.ones((ONES_ROWS, vwin_t.shape[1]), BF16)

    def scores(j, g):
        qj = q_ref[j * SUB:(j + 1) * SUB, :]
        prev_var = jnp.where(first, 1, 0) if j == 0 else 0
        next_var = jnp.where(last, 4, 3) if j == n_sub - 1 else 3
        qg = jnp.concatenate(
            [qj[:, (g * REP + r) * HEAD_DIM:(g * REP + r + 1) * HEAD_DIM] for r in range(REP)], axis=0)
        kg = kwin[j * SUB:(j + 3) * SUB, g * HEAD_DIM:(g + 1) * HEAD_DIM]
        s = lax.dot_general(kg, qg, (((1,), (1,)), ((), ())), preferred_element_type=F32)
        bias = jnp.concatenate([bias_ref[prev_var, g], bias_ref[2, g], bias_ref[next_var, g]], axis=0)
        s = s + bias
        sink = jnp.concatenate(
            [jnp.full((1, SUB), sink_ref[g * REP + r], F32) for r in range(REP)], axis=1)
        m = jnp.maximum(jnp.max(s, axis=0, keepdims=True), sink)
        return s, m, sink

    def probs(s, m):
        return jnp.exp2(s - m).astype(BF16)

    def weighted_values(j, g, p, m, sink):
        vg_t = jnp.concatenate([vwin_t[g * HEAD_DIM:(g + 1) * HEAD_DIM], ones], axis=0)
        vg_t = vg_t[:, j * SUB:(j + 3) * SUB]
        ov = jnp.dot(vg_t, p, preferred_element_type=F32)
        denom = ov[HEAD_DIM:HEAD_DIM + 1] + jnp.exp2(sink - m)
        return ov[:HEAD_DIM] / denom

    units = [(j, g) for j in range(n_sub) for g in range(N_KV_HEADS)]
    stage1, stage2 = {}, {}
    heads_t = {j: [] for j in range(n_sub)}
    for step in range(len(units) + 2):
        if step >= 2:
            j, g = units[step - 2]
            p, m, sink = stage2.pop(step - 2)
            og_t = weighted_values(j, g, p, m, sink)
            heads_t[j] += [og_t[:, r * SUB:(r + 1) * SUB] for r in range(REP)]
            if g == N_KV_HEADS - 1:
                o_ref[j * SUB:(j + 1) * SUB, :] = jnp.concatenate(heads_t.pop(j), axis=0).T.astype(BF16)
        if 1 <= step <= len(units):
            s, m, sink = stage1.pop(step - 1)
            stage2[step - 1] = (probs(s, m), m, sink)
        if step < len(units):
            stage1[step] = scores(*units[step])


def _attention(q, k, v, sink, B, S):
    T, d_attn = q.shape
    d_kv = k.shape[1]
    tq = TOKEN_TILE
    n_sub = tq // SUB
    nq = S // tq
    nblk = S // SUB
    bias = jnp.asarray(_attention_bias())

    def q_map(b, n):
        return (b * nq + n, 0)

    def prev_map(b, n):
        return (b * nblk + jnp.maximum(n * n_sub - 1, 0), 0)

    def next_map(b, n):
        return (b * nblk + jnp.minimum((n + 1) * n_sub, nblk - 1), 0)

    kv_specs = [pl.BlockSpec((SUB, d_kv), prev_map), pl.BlockSpec((tq, d_kv), q_map),
                pl.BlockSpec((SUB, d_kv), next_map)]
    return pl.pallas_call(
        functools.partial(_attention_body, n_sub),
        grid=(B, nq),
        in_specs=[pl.BlockSpec(memory_space=pltpu.SMEM),
                  pl.BlockSpec((tq, d_attn), q_map)] + kv_specs + kv_specs + [_const_spec(bias.shape)],
        out_specs=pl.BlockSpec((tq, d_attn), q_map),
        out_shape=jax.ShapeDtypeStruct((T, d_attn), BF16),
        compiler_params=_params(("parallel", "parallel")),
        name="attention",
    )(sink, q, k, k, k, v, v, v, bias)


HALO = 16
MIXER_PARTS = 2


def _mixer_out_body(tiles_per_seq, o_ref, cb_ref, cc_ref, cx_ref, ccp_ref, cxp_ref, ccn_ref, cxn_ref,
                    ga_ref, gc_ref, xn_ref, cw_ref, wa_ref, wc_ref, wo_ref, g_ref, b_ref, wr_ref,
                    x1_ref, x1b_ref, lg_ref):
    i = pl.program_id(0)
    n = i % tiles_per_seq
    tm = cc_ref.shape[0]
    pm = tm // MIXER_PARTS
    u = cc_ref[...].astype(F32) * cx_ref[...].astype(F32)
    u_before = ccp_ref[HALO - 1:HALO, :].astype(F32) * cxp_ref[HALO - 1:HALO, :].astype(F32)
    u_after = ccn_ref[0:1, :].astype(F32) * cxn_ref[0:1, :].astype(F32)
    u_before = jnp.where(n == 0, 0.0, u_before)
    u_after = jnp.where(n == tiles_per_seq - 1, 0.0, u_after)
    cw = cw_ref[...]
    row = lax.broadcasted_iota(jnp.int32, (pm, u.shape[1]), 0)

    def project_in(p):
        r = pl.ds(p * pm, pm)
        y_attn = jnp.dot(o_ref[r, :], wa_ref[...], preferred_element_type=F32)
        up = u[p * pm:(p + 1) * pm]
        before = u_before if p == 0 else u[p * pm - 1:p * pm]
        after = u_after if p == MIXER_PARTS - 1 else u[(p + 1) * pm:(p + 1) * pm + 1]
        um1 = jnp.where(row == 0, before, pltpu.roll(up, 1, 0))
        up1 = jnp.where(row == pm - 1, after, pltpu.roll(up, pm - 1, 0))
        y = cb_ref[r, :].astype(F32) * (cw[0:1, :] * um1 + cw[1:2, :] * up + cw[2:3, :] * up1)
        y_conv = jnp.dot(y.astype(BF16), wc_ref[...], preferred_element_type=F32)
        return y_attn, y_conv

    def project_out(p, y_attn, y_conv):
        r = pl.ds(p * pm, pm)
        merged = (jax.nn.sigmoid(ga_ref[r, :].astype(F32)) * y_attn
                  + jax.nn.sigmoid(gc_ref[r, :].astype(F32)) * y_conv)
        return jnp.dot(merged.astype(BF16), wo_ref[...], preferred_element_type=F32)

    def finish(p, h):
        r = pl.ds(p * pm, pm)
        x1 = _layer_norm(DEEPNORM_ALPHA * xn_ref[r, :] + h, g_ref[...], b_ref[...])
        x1_ref[r, :] = x1
        x1_hi = x1.astype(BF16)
        x1b_ref[r, :] = x1_hi
        x1_lo = (x1 - x1_hi.astype(F32)).astype(BF16)
        nt = (((1,), (1,)), ((), ()))
        E = lg_ref.shape[0]
        by_hi = lax.dot_general(wr_ref[...], x1_hi, nt, preferred_element_type=F32)
        by_lo = lax.dot_general(wr_ref[0:E, :], x1_lo, nt, preferred_element_type=F32)
        lg_ref[:, r] = by_hi[0:E] + (by_hi[E:2 * E] + by_lo)

    stage1, stage2 = {}, {}
    for step in range(MIXER_PARTS + 2):
        if step < MIXER_PARTS:
            stage1[step] = project_in(step)
        if 0 <= step - 1 < MIXER_PARTS:
            stage2[step - 1] = project_out(step - 1, *stage1.pop(step - 1))
        if 0 <= step - 2 < MIXER_PARTS:
            finish(step - 2, stage2.pop(step - 2))


def _mixer_out(o, cb, cc, cx, ga, gc, xn, conv_w, wa, wc, wo, g, b, wr_t, S):
    T, D = xn.shape
    tm = TOKEN_TILE
    tiles_per_seq = S // tm
    hb = tm // HALO
    n_halo = T // HALO
    E = wr_t.shape[0] // 2

    def tile(i):
        return (i, 0)

    def before(i):
        return (jnp.maximum(i * hb - 1, 0), 0)

    def after(i):
        return (jnp.minimum((i + 1) * hb, n_halo - 1), 0)

    tspec = pl.BlockSpec((tm, D), tile)
    return pl.pallas_call(
        functools.partial(_mixer_out_body, tiles_per_seq),
        grid=(T // tm,),
        in_specs=[tspec, tspec, tspec, tspec,
                  pl.BlockSpec((HALO, D), before), pl.BlockSpec((HALO, D), before),
                  pl.BlockSpec((HALO, D), after), pl.BlockSpec((HALO, D), after),
                  tspec, tspec, tspec,
                  _const_spec(conv_w.shape), _const_spec(wa.shape), _const_spec(wc.shape),
                  _const_spec(wo.shape), _const_spec(g.shape), _const_spec(b.shape),
                  _const_spec(wr_t.shape)],
        out_specs=[tspec, tspec, pl.BlockSpec((E, tm), lambda i: (0, i))],
        out_shape=[jax.ShapeDtypeStruct((T, D), F32), jax.ShapeDtypeStruct((T, D), BF16),
                   jax.ShapeDtypeStruct((E, T), F32)],
        compiler_params=_params(("parallel",)),
        name="mixer_out",
    )(o, cb, cc, cx, cc, cx, cc, cx, ga, gc, xn, conv_w, wa, wc, wo, g, b, wr_t)


VALUE_BITS = 32
RADIX_BITS = 3
SCAN = 256
CHUNK = 256
WINDOW_ROWS = 80


def _exclusive_count(mask, tri):
    E, S = mask.shape
    mb = mask.astype(BF16)
    carry = jnp.zeros((E, 1), F32)
    parts = []
    for c in range(0, S, SCAN):
        chunk = mb[:, c:c + SCAN]
        parts.append(jnp.dot(chunk, tri, preferred_element_type=F32) + carry)
        carry = carry + jnp.sum(chunk.astype(F32), axis=1, keepdims=True)
    return jnp.concatenate(parts, axis=1)


def _route_body(cap, lg_ref, tri_ref, slot_ref, aff_ref, start_ref):
    logits = lg_ref[...]
    mx = jnp.max(logits, axis=0, keepdims=True)
    ex = jnp.exp(logits - mx)
    aff = ex / jnp.sum(ex, axis=0, keepdims=True)
    aff_ref[...] = aff
    bits = pltpu.bitcast(aff, jnp.int32)
    E = bits.shape[0]

    def count_ge(cand):
        return jnp.sum((bits >= cand).astype(F32), axis=1, keepdims=True)

    def step(k, prefix):
        shift = VALUE_BITS - 1 - RADIX_BITS * (k + 1)
        out = prefix
        for digit in range(1, 2 ** RADIX_BITS):
            cand = prefix | lax.shift_left(jnp.int32(digit), shift)
            out = jnp.where(count_ge(cand) >= cap, cand, out)
        return out

    top = jnp.full((E, 1), 1 << (VALUE_BITS - 2), jnp.int32)
    prefix = jnp.where(count_ge(top) >= cap, top, 0)
    thresh = lax.fori_loop(0, (VALUE_BITS - 2) // RADIX_BITS, step, prefix)
    gt = bits > thresh
    eq = bits == thresh
    need = cap - jnp.sum(gt.astype(F32), axis=1, keepdims=True)
    tri = tri_ref[...]
    sel = gt | (eq & (_exclusive_count(eq, tri) < need))
    pos = _exclusive_count(sel, tri)
    slot_ref[...] = jnp.where(sel, pos, -1.0).astype(jnp.int32)
    S = pos.shape[1]
    start_ref[0] = jnp.concatenate([pos[:, c:c + 1] for c in range(0, S, CHUNK)], axis=1).astype(jnp.int32)


def _route(logits_t, B, S, cap):
    E = logits_t.shape[0]
    tri = jnp.asarray(np.triu(np.ones((SCAN, SCAN), np.float32), 1), BF16)
    spec = pl.BlockSpec((E, S), lambda b: (0, b))
    return pl.pallas_call(
        functools.partial(_route_body, cap),
        grid=(B,),
        in_specs=[spec, _const_spec(tri.shape)],
        out_specs=[spec, spec, pl.BlockSpec((1, E, S // CHUNK), lambda b: (b, 0, 0))],
        out_shape=[jax.ShapeDtypeStruct((E, B * S), jnp.int32), jax.ShapeDtypeStruct((E, B * S), F32),
                   jax.ShapeDtypeStruct((B, E, S // CHUNK), jnp.int32)],
        compiler_params=_params(("parallel",)),
        name="route",
    )(logits_t, tri)


def _window(starts_ref, first, kt, n_chunks, cap):
    start = starts_ref[first + kt]
    nxt = starts_ref[first + jnp.minimum(kt + 1, n_chunks - 1)]
    end = jnp.where(kt + 1 < n_chunks, nxt, cap)
    base = jnp.minimum(lax.shift_left(lax.shift_right_logical(start, 4), 4), cap - WINDOW_ROWS)
    return pl.multiple_of(base, 16), end <= base + WINDOW_ROWS


def _dispatch_body(cap, starts_ref, slot_ref, aff_ref, x_ref, xe_ref, tv_ref):
    b = pl.program_id(0)
    E, S = slot_ref.shape
    n_chunks = S // CHUNK
    windows = [[_window(starts_ref, (b * E + e) * n_chunks, kt, n_chunks, cap) for e in range(E)]
               for kt in range(n_chunks)]
    fits = functools.reduce(jnp.logical_and, [w[1] for per_chunk in windows for w in per_chunk])

    @pl.when(fits)
    def _():
        xe_ref[...] = jnp.zeros_like(xe_ref)
        tv_ref[...] = jnp.zeros_like(tv_ref)
        ri = lax.broadcasted_iota(jnp.int32, (WINDOW_ROWS, CHUNK), 0)
        for kt in range(n_chunks):
            cols = slice(kt * CHUNK, (kt + 1) * CHUNK)
            hits = [(ri + windows[kt][e][0]) == slot_ref[e:e + 1, cols] for e in range(E)]
            onehot = jnp.concatenate([jnp.where(h, 1.0, 0.0).astype(BF16) for h in hits], axis=0)
            rows_all = jnp.dot(onehot, x_ref[cols, :], preferred_element_type=F32)
            for e in range(E):
                rows = pl.ds(windows[kt][e][0], WINDOW_ROWS)
                xe_ref[0, e, rows, :] += rows_all[e * WINDOW_ROWS:(e + 1) * WINDOW_ROWS].astype(BF16)
                tv = jnp.sum(jnp.where(hits[e], aff_ref[e:e + 1, cols], 0.0), axis=1, keepdims=True)
                tv_ref[0, e, rows, :] += jnp.broadcast_to(tv, (WINDOW_ROWS, LANES))

    @pl.when(jnp.logical_not(fits))
    def _():
        ri = lax.broadcasted_iota(jnp.int32, (cap, S), 0)
        for e in range(E):
            hit = ri == slot_ref[e:e + 1, :]
            onehot = jnp.where(hit, 1.0, 0.0).astype(BF16)
            xe_ref[0, e] = jnp.dot(onehot, x_ref[...], preferred_element_type=F32).astype(BF16)
            tv = jnp.sum(jnp.where(hit, aff_ref[e:e + 1, :], 0.0), axis=1, keepdims=True)
            tv_ref[0, e] = jnp.broadcast_to(tv, (cap, LANES))


def _dispatch(starts, slot, aff, x1b, B, S, cap):
    E = slot.shape[0]
    D = x1b.shape[1]
    row = pl.BlockSpec((E, S), lambda b, st: (0, b))
    return pl.pallas_call(
        functools.partial(_dispatch_body, cap),
        grid_spec=pltpu.PrefetchScalarGridSpec(
            num_scalar_prefetch=1,
            grid=(B,),
            in_specs=[row, row, pl.BlockSpec((S, D), lambda b, st: (b, 0))],
            out_specs=[pl.BlockSpec((1, E, cap, D), lambda b, st: (b, 0, 0, 0)),
                       pl.BlockSpec((1, E, cap, LANES), lambda b, st: (b, 0, 0, 0))]),
        out_shape=[jax.ShapeDtypeStruct((B, E, cap, D), BF16),
                   jax.ShapeDtypeStruct((B, E, cap, LANES), F32)],
        compiler_params=_params(("arbitrary",)),
        name="dispatch",
    )(starts, slot, aff, x1b)


EXPERT_SEQS = 4


def _experts_body(xe_ref, tv_ref, wg_ref, wu_ref, wd_ref, y_ref):
    nb, _, cap, D = xe_ref.shape
    xe = xe_ref[...].reshape(nb * cap, D)
    tv = tv_ref[...].reshape(nb * cap, LANES)[:, 0:1]
    gate = jnp.dot(xe, wg_ref[0].astype(BF16), preferred_element_type=F32)
    up = jnp.dot(xe, wu_ref[0].astype(BF16), preferred_element_type=F32)
    h = (gate * jax.nn.sigmoid(gate) * up).astype(BF16)
    ye = jnp.dot(h, wd_ref[0].astype(BF16), preferred_element_type=F32) * tv
    y_ref[...] = ye.astype(BF16).reshape(nb, 1, cap, D)


def _experts(xe, tv, wg, wu, wd):
    B, E, cap, D = xe.shape
    F = wg.shape[2]
    nb = EXPERT_SEQS
    wspec = lambda shape: pl.BlockSpec(shape, lambda e, b: (e, 0, 0), pipeline_mode=pl.Buffered(1))
    tok = lambda width: pl.BlockSpec((nb, 1, cap, width), lambda e, b: (b, e, 0, 0))
    return pl.pallas_call(
        _experts_body,
        grid=(E, B // nb),
        in_specs=[tok(D), tok(LANES), wspec((1, D, F)), wspec((1, D, F)), wspec((1, F, D))],
        out_specs=tok(D),
        out_shape=jax.ShapeDtypeStruct((B, E, cap, D), BF16),
        compiler_params=_params(("arbitrary", "arbitrary")),
        name="experts",
    )(xe, tv, wg, wu, wd)


def _combine_body(cap, tiles_per_seq, starts_ref, slot_ref, y_ref, x1_ref, g_ref, b_ref, o_ref, f_ref):
    i = pl.program_id(0)
    E = slot_ref.shape[0]
    tm = slot_ref.shape[1]
    subs = tm // CHUNK
    n_chunks = tiles_per_seq * subs
    b = i // tiles_per_seq
    nt = (((0,), (0,)), ((), ()))
    for sub in range(subs):
        kt = (i % tiles_per_seq) * subs + sub
        cols = slice(sub * CHUNK, (sub + 1) * CHUNK)
        windows = [_window(starts_ref, (b * E + e) * n_chunks, kt, n_chunks, cap) for e in range(E)]
        fits = functools.reduce(jnp.logical_and, [w[1] for w in windows])

        @pl.when(fits)
        def _():
            ri = lax.broadcasted_iota(jnp.int32, (WINDOW_ROWS, CHUNK), 0)
            onehot = jnp.concatenate(
                [jnp.where(ri + windows[e][0] == slot_ref[e:e + 1, cols], 1.0, 0.0).astype(BF16)
                 for e in range(E)], axis=0)
            ywin = jnp.concatenate(
                [y_ref[0, e, pl.ds(windows[e][0], WINDOW_ROWS), :] for e in range(E)], axis=0)
            f_ref[cols, :] = lax.dot_general(onehot, ywin, nt, preferred_element_type=F32)

        @pl.when(jnp.logical_not(fits))
        def _():
            ri = lax.broadcasted_iota(jnp.int32, (cap, CHUNK), 0)
            onehot = jnp.concatenate(
                [jnp.where(ri == slot_ref[e:e + 1, cols], 1.0, 0.0).astype(BF16) for e in range(E)], axis=0)
            yall = jnp.concatenate([y_ref[0, e] for e in range(E)], axis=0)
            f_ref[cols, :] = lax.dot_general(onehot, yall, nt, preferred_element_type=F32)

    o_ref[...] = _layer_norm(DEEPNORM_ALPHA * x1_ref[...] + f_ref[...], g_ref[...], b_ref[...])


def _combine(starts, slot, y, x1, g, b, B, S, cap):
    T, D = x1.shape
    E = slot.shape[0]
    tm = TOKEN_TILE
    tiles_per_seq = S // tm
    const = lambda shape: pl.BlockSpec(shape, lambda i, st: (0,) * len(shape), pipeline_mode=pl.Buffered(1))
    return pl.pallas_call(
        functools.partial(_combine_body, cap, tiles_per_seq),
        grid_spec=pltpu.PrefetchScalarGridSpec(
            num_scalar_prefetch=1,
            grid=(T // tm,),
            in_specs=[pl.BlockSpec((E, tm), lambda i, st: (0, i)),
                      pl.BlockSpec((1, E, cap, D), lambda i, st: (i // tiles_per_seq, 0, 0, 0)),
                      pl.BlockSpec((tm, D), lambda i, st: (i, 0)),
                      const(g.shape), const(b.shape)],
            out_specs=pl.BlockSpec((tm, D), lambda i, st: (i, 0)),
            scratch_shapes=[pltpu.VMEM((tm, D), F32)]),
        out_shape=jax.ShapeDtypeStruct((T, D), F32),
        compiler_params=_params(("arbitrary",)),
        name="combine",
    )(starts, slot, y, x1, g, b)


def _split_hi_lo(w):
    hi = w.astype(BF16)
    lo = (w - hi.astype(F32)).astype(BF16)
    return jnp.concatenate([hi, lo], axis=0)


def kernel(x, ln0_g, ln0_b, w_in, conv_w, attn_sink, w_attn_o, w_conv_o, w_out, ln1_g, ln1_b,
           w_router, w_gate, w_up, w_down, ln2_g, ln2_b):
    B, S, D = x.shape
    assert w_in.shape[0] == DEPTH == 1
    assert S % TOKEN_TILE == 0 and TOKEN_TILE % SUB == 0 and TOKEN_TILE % CHUNK == 0
    assert B % EXPERT_SEQS == 0
    T = B * S
    d_attn = N_HEADS * HEAD_DIM
    d_kv = N_KV_HEADS * HEAD_DIM
    cap = CAPACITY_FACTOR * S // N_EXPERTS
    row = lambda a: a.reshape(1, -1).astype(F32)

    splits = ((d_attn, HEAD_DIM ** -0.5 * LOG2E), (d_kv, 1.0), (d_kv, 1.0),
              (D, 1.0), (D, 1.0), (D, 1.0), (D, 1.0), (D, 1.0))
    xn, q, k, v, cb, cc, cx, ga, gc = _inproj(
        x.reshape(T, D), row(ln0_g), row(ln0_b), w_in[0].astype(BF16), splits)

    o = _attention(q, k, v, attn_sink[0].astype(F32) * LOG2E, B, S)

    x1, x1b, logits_t = _mixer_out(
        o, cb, cc, cx, ga, gc, xn, conv_w[0].astype(F32),
        w_attn_o[0].astype(BF16), w_conv_o[0].astype(BF16), w_out[0].astype(BF16),
        row(ln1_g[0]), row(ln1_b[0]), _split_hi_lo(w_router[0].T.astype(F32)), S)

    slot, aff, starts = _route(logits_t, B, S, cap)
    starts = starts.reshape(-1)

    xe, top_val = _dispatch(starts, slot, aff, x1b, B, S, cap)

    y = _experts(xe, top_val, w_gate[0], w_up[0], w_down[0])

    out = _combine(starts, slot, y, x1, row(ln2_g[0]), row(ln2_b[0]), B, S, cap)
    return out.reshape(B, S, D)
```

```python
import functools

import jax
import jax.numpy as jnp
import numpy as np
from jax import lax
from jax.experimental import pallas as pl
from jax.experimental.pallas import tpu as pltpu

F32 = jnp.float32
BF16 = jnp.bfloat16

N_HEADS = 16
N_KV_HEADS = 4
HEAD_DIM = 64
REP = N_HEADS // N_KV_HEADS
WINDOW = 128
BLOCK = 128
NEG_INF = -1e30
LOG2E = float(np.log2(np.e))
CONV_WIDTH = 3
N_EXPERTS = 16
CAPACITY_FACTOR = 2
DEPTH = 1
DEEPNORM_ALPHA = (2.0 * DEPTH) ** 0.25
LN_EPS = 1e-5

V7X_VMEM_BYTES = 64 * 1024 * 1024
LANES = 128
VMEM_LIMIT = 56 * 1024 * 1024

TOKEN_TILE = 512
SUB = BLOCK


def _layer_norm(x, g, b):
    mu = jnp.mean(x, axis=-1, keepdims=True)
    xc = x - mu
    var = jnp.mean(xc * xc, axis=-1, keepdims=True)
    return xc * lax.rsqrt(var + LN_EPS) * g + b


def _params(sem):
    return pltpu.CompilerParams(dimension_semantics=sem, vmem_limit_bytes=VMEM_LIMIT)


def _const_spec(shape):
    zeros = (0,) * len(shape)
    return pl.BlockSpec(shape, lambda *_: zeros, pipeline_mode=pl.Buffered(1))


def _inproj_body(splits, x_ref, g_ref, b_ref, w_ref, xn_ref, *out_refs):
    xn = _layer_norm(x_ref[...], g_ref[...], b_ref[...])
    xn_ref[...] = xn
    xb = xn.astype(BF16)
    off = 0
    for (width, scale), o_ref in zip(splits, out_refs):
        chunk = min(width, 512)
        for c in range(0, width, chunk):
            acc = jnp.dot(xb, w_ref[:, off + c:off + c + chunk], preferred_element_type=F32)
            if scale != 1.0:
                acc = acc * scale
            o_ref[:, c:c + chunk] = acc.astype(BF16)
        off += width


def _inproj(x2, g, b, w_bf16, splits):
    T, D = x2.shape
    n_in = w_bf16.shape[1]
    tm = TOKEN_TILE
    out_shape = [jax.ShapeDtypeStruct((T, D), F32)]
    out_specs = [pl.BlockSpec((tm, D), lambda i: (i, 0))]
    for width, _ in splits:
        out_shape.append(jax.ShapeDtypeStruct((T, width), BF16))
        out_specs.append(pl.BlockSpec((tm, width), lambda i: (i, 0)))
    return pl.pallas_call(
        functools.partial(_inproj_body, splits),
        grid=(T // tm,),
        in_specs=[
            pl.BlockSpec((tm, D), lambda i: (i, 0)),
            _const_spec((1, D)),
            _const_spec((1, D)),
            _const_spec((D, n_in)),
        ],
        out_specs=out_specs,
        out_shape=out_shape,
        compiler_params=_params(("parallel",)),
        name="inproj",
    )(x2, g, b, w_bf16)


def _alibi_slopes():
    return (2.0 ** (-8.0 * np.arange(1, N_HEADS + 1) / N_HEADS)).astype(np.float32)


def _attention_bias():
    slopes = _alibi_slopes().reshape(N_KV_HEADS, REP)
    i = np.arange(SUB)[None, :]
    j = np.arange(SUB)[:, None]
    out = np.empty((5, N_KV_HEADS, SUB, REP * SUB), np.float32)
    for var, (shift, masked) in enumerate([(-SUB, False), (-SUB, True), (0, False), (SUB, False), (SUB, True)]):
        dist = np.abs(i - (j + shift))
        valid = (dist <= WINDOW) & (not masked)
        for g in range(N_KV_HEADS):
            for r in range(REP):
                b = (-slopes[g, r] * dist.astype(np.float32)).astype(np.float64) * LOG2E
                out[var, g, :, r * SUB:(r + 1) * SUB] = np.where(valid, b, NEG_INF).astype(np.float32)
    return out


ONES_ROWS = 16


def _attention_body(n_sub, sink_ref, q_ref, kp_ref, kc_ref, kn_ref, vp_ref, vc_ref, vn_ref,
                    bias_ref, o_ref):
    n = pl.program_id(1)
    first = n == 0
    last = n == pl.num_programs(1) - 1
    kwin = jnp.concatenate([kp_ref[...], kc_ref[...], kn_ref[...]], axis=0)
    vwin = jnp.concatenate([vp_ref[...], vc_ref[...], vn_ref[...]], axis=0)
    vwin_t = vwin.astype(F32).T.astype(BF16)
    ones = jnp.ones((ONES_ROWS, vwin_t.shape[1]), BF16)

    def scores(j, g):
        qj = q_ref[j * SUB:(j + 1) * SUB, :]
        prev_var = jnp.where(first, 1, 0) if j == 0 else 0
        next_var = jnp.where(last, 4, 3) if j == n_sub - 1 else 3
        qg = jnp.concatenate(
            [qj[:, (g * REP + r) * HEAD_DIM:(g * REP + r + 1) * HEAD_DIM] for r in range(REP)], axis=0)
        kg = kwin[j * SUB:(j + 3) * SUB, g * HEAD_DIM:(g + 1) * HEAD_DIM]
        s = lax.dot_general(kg, qg, (((1,), (1,)), ((), ())), preferred_element_type=F32)
        bias = jnp.concatenate([bias_ref[prev_var, g], bias_ref[2, g], bias_ref[next_var, g]], axis=0)
        s = s + bias
        sink = jnp.concatenate(
            [jnp.full((1, SUB), sink_ref[g * REP + r], F32) for r in range(REP)], axis=1)
        m = jnp.maximum(jnp.max(s, axis=0, keepdims=True), sink)
        return s, m, sink

    def probs(s, m):
        return jnp.exp2(s - m).astype(BF16)

    def weighted_values(j, g, p, m, sink):
        vg_t = jnp.concatenate([vwin_t[g * HEAD_DIM:(g + 1) * HEAD_DIM], ones], axis=0)
        vg_t = vg_t[:, j * SUB:(j + 3) * SUB]
        ov = jnp.dot(vg_t, p, preferred_element_type=F32)
        denom = ov[HEAD_DIM:HEAD_DIM + 1] + jnp.exp2(sink - m)
        return ov[:HEAD_DIM] / denom

    units = [(j, g) for j in range(n_sub) for g in range(N_KV_HEADS)]
    stage1, stage2 = {}, {}
    heads_t = {j: [] for j in range(n_sub)}
    for step in range(len(units) + 2):
        if step >= 2:
            j, g = units[step - 2]
            p, m, sink = stage2.pop(step - 2)
            og_t = weighted_values(j, g, p, m, sink)
            heads_t[j] += [og_t[:, r * SUB:(r + 1) * SUB] for r in range(REP)]
            if g == N_KV_HEADS - 1:
                o_ref[j * SUB:(j + 1) * SUB, :] = jnp.concatenate(heads_t.pop(j), axis=0).T.astype(BF16)
        if 1 <= step <= len(units):
            s, m, sink = stage1.pop(step - 1)
            stage2[step - 1] = (probs(s, m), m, sink)
        if step < len(units):
            stage1[step] = scores(*units[step])


def _attention(q, k, v, sink, B, S):
    T, d_attn = q.shape
    d_kv = k.shape[1]
    tq = TOKEN_TILE
    n_sub = tq // SUB
    nq = S // tq
    nblk = S // SUB
    bias = jnp.asarray(_attention_bias())

    def q_map(b, n):
        return (b * nq + n, 0)

    def prev_map(b, n):
        return (b * nblk + jnp.maximum(n * n_sub - 1, 0), 0)

    def next_map(b, n):
        return (b * nblk + jnp.minimum((n + 1) * n_sub, nblk - 1), 0)

    kv_specs = [pl.BlockSpec((SUB, d_kv), prev_map), pl.BlockSpec((tq, d_kv), q_map),
                pl.BlockSpec((SUB, d_kv), next_map)]
    return pl.pallas_call(
        functools.partial(_attention_body, n_sub),
        grid=(B, nq),
        in_specs=[pl.BlockSpec(memory_space=pltpu.SMEM),
                  pl.BlockSpec((tq, d_attn), q_map)] + kv_specs + kv_specs + [_const_spec(bias.shape)],
        out_specs=pl.BlockSpec((tq, d_attn), q_map),
        out_shape=jax.ShapeDtypeStruct((T, d_attn), BF16),
        compiler_params=_params(("parallel", "parallel")),
        name="attention",
    )(sink, q, k, k, k, v, v, v, bias)


HALO = 16
MIXER_PARTS = 2


def _mixer_out_body(tiles_per_seq, o_ref, cb_ref, cc_ref, cx_ref, ccp_ref, cxp_ref, ccn_ref, cxn_ref,
                    ga_ref, gc_ref, xn_ref, cw_ref, wa_ref, wc_ref, wo_ref, g_ref, b_ref, wr_ref,
                    x1_ref, x1b_ref, lg_ref):
    i = pl.program_id(0)
    n = i % tiles_per_seq
    tm = cc_ref.shape[0]
    pm = tm // MIXER_PARTS
    u = cc_ref[...].astype(F32) * cx_ref[...].astype(F32)
    u_before = ccp_ref[HALO - 1:HALO, :].astype(F32) * cxp_ref[HALO - 1:HALO, :].astype(F32)
    u_after = ccn_ref[0:1, :].astype(F32) * cxn_ref[0:1, :].astype(F32)
    u_before = jnp.where(n == 0, 0.0, u_before)
    u_after = jnp.where(n == tiles_per_seq - 1, 0.0, u_after)
    cw = cw_ref[...]
    row = lax.broadcasted_iota(jnp.int32, (pm, u.shape[1]), 0)

    def project_in(p):
        r = pl.ds(p * pm, pm)
        y_attn = jnp.dot(o_ref[r, :], wa_ref[...], preferred_element_type=F32)
        up = u[p * pm:(p + 1) * pm]
        before = u_before if p == 0 else u[p * pm - 1:p * pm]
        after = u_after if p == MIXER_PARTS - 1 else u[(p + 1) * pm:(p + 1) * pm + 1]
        um1 = jnp.where(row == 0, before, pltpu.roll(up, 1, 0))
        up1 = jnp.where(row == pm - 1, after, pltpu.roll(up, pm - 1, 0))
        y = cb_ref[r, :].astype(F32) * (cw[0:1, :] * um1 + cw[1:2, :] * up + cw[2:3, :] * up1)
        y_conv = jnp.dot(y.astype(BF16), wc_ref[...], preferred_element_type=F32)
        return y_attn, y_conv

    def project_out(p, y_attn, y_conv):
        r = pl.ds(p * pm, pm)
        merged = (jax.nn.sigmoid(ga_ref[r, :].astype(F32)) * y_attn
                  + jax.nn.sigmoid(gc_ref[r, :].astype(F32)) * y_conv)
        return jnp.dot(merged.astype(BF16), wo_ref[...], preferred_element_type=F32)

    def finish(p, h):
        r = pl.ds(p * pm, pm)
        x1 = _layer_norm(DEEPNORM_ALPHA * xn_ref[r, :] + h, g_ref[...], b_ref[...])
        x1_ref[r, :] = x1
        x1_hi = x1.astype(BF16)
        x1b_ref[r, :] = x1_hi
        x1_lo = (x1 - x1_hi.astype(F32)).astype(BF16)
        nt = (((1,), (1,)), ((), ()))
        E = lg_ref.shape[0]
        by_hi = lax.dot_general(wr_ref[...], x1_hi, nt, preferred_element_type=F32)
        by_lo = lax.dot_general(wr_ref[0:E, :], x1_lo, nt, preferred_element_type=F32)
        lg_ref[:, r] = by_hi[0:E] + (by_hi[E:2 * E] + by_lo)

    stage1, stage2 = {}, {}
    for step in range(MIXER_PARTS + 2):
        if step < MIXER_PARTS:
            stage1[step] = project_in(step)
        if 0 <= step - 1 < MIXER_PARTS:
            stage2[step - 1] = project_out(step - 1, *stage1.pop(step - 1))
        if 0 <= step - 2 < MIXER_PARTS:
            finish(step - 2, stage2.pop(step - 2))


def _mixer_out(o, cb, cc, cx, ga, gc, xn, conv_w, wa, wc, wo, g, b, wr_t, S):
    T, D = xn.shape
    tm = TOKEN_TILE
    tiles_per_seq = S // tm
    hb = tm // HALO
    n_halo = T // HALO
    E = wr_t.shape[0] // 2

    def tile(i):
        return (i, 0)

    def before(i):
        return (jnp.maximum(i * hb - 1, 0), 0)

    def after(i):
        return (jnp.minimum((i + 1) * hb, n_halo - 1), 0)

    tspec = pl.BlockSpec((tm, D), tile)
    return pl.pallas_call(
        functools.partial(_mixer_out_body, tiles_per_seq),
        grid=(T // tm,),
        in_specs=[tspec, tspec, tspec, tspec,
                  pl.BlockSpec((HALO, D), before), pl.BlockSpec((HALO, D), before),
                  pl.BlockSpec((HALO, D), after), pl.BlockSpec((HALO, D), after),
                  tspec, tspec, tspec,
                  _const_spec(conv_w.shape), _const_spec(wa.shape), _const_spec(wc.shape),
                  _const_spec(wo.shape), _const_spec(g.shape), _const_spec(b.shape),
                  _const_spec(wr_t.shape)],
        out_specs=[tspec, tspec, pl.BlockSpec((E, tm), lambda i: (0, i))],
        out_shape=[jax.ShapeDtypeStruct((T, D), F32), jax.ShapeDtypeStruct((T, D), BF16),
                   jax.ShapeDtypeStruct((E, T), F32)],
        compiler_params=_params(("parallel",)),
        name="mixer_out",
    )(o, cb, cc, cx, cc, cx, cc, cx, ga, gc, xn, conv_w, wa, wc, wo, g, b, wr_t)


VALUE_BITS = 32
RADIX_BITS = 3
SCAN = 256
CHUNK = 256
WINDOW_ROWS = 80


def _exclusive_count(mask, tri):
    E, S = mask.shape
    mb = mask.astype(BF16)
    carry = jnp.zeros((E, 1), F32)
    parts = []
    for c in range(0, S, SCAN):
        chunk = mb[:, c:c + SCAN]
        parts.append(jnp.dot(chunk, tri, preferred_element_type=F32) + carry)
        carry = carry + jnp.sum(chunk.astype(F32), axis=1, keepdims=True)
    return jnp.concatenate(parts, axis=1)


def _route_body(cap, lg_ref, tri_ref, slot_ref, aff_ref, start_ref):
    logits = lg_ref[...]
    mx = jnp.max(logits, axis=0, keepdims=True)
    ex = jnp.exp(logits - mx)
    aff = ex / jnp.sum(ex, axis=0, keepdims=True)
    aff_ref[...] = aff
    bits = pltpu.bitcast(aff, jnp.int32)
    E = bits.shape[0]

    def count_ge(cand):
        return jnp.sum((bits >= cand).astype(F32), axis=1, keepdims=True)

    def step(k, prefix):
        shift = VALUE_BITS - 1 - RADIX_BITS * (k + 1)
        out = prefix
        for digit in range(1, 2 ** RADIX_BITS):
            cand = prefix | lax.shift_left(jnp.int32(digit), shift)
            out = jnp.where(count_ge(cand) >= cap, cand, out)
        return out

    top = jnp.full((E, 1), 1 << (VALUE_BITS - 2), jnp.int32)
    prefix = jnp.where(count_ge(top) >= cap, top, 0)
    thresh = lax.fori_loop(0, (VALUE_BITS - 2) // RADIX_BITS, step, prefix)
    gt = bits > thresh
    eq = bits == thresh
    need = cap - jnp.sum(gt.astype(F32), axis=1, keepdims=True)
    tri = tri_ref[...]
    sel = gt | (eq & (_exclusive_count(eq, tri) < need))
    pos = _exclusive_count(sel, tri)
    slot_ref[...] = jnp.where(sel, pos, -1.0).astype(jnp.int32)
    S = pos.shape[1]
    start_ref[0] = jnp.concatenate([pos[:, c:c + 1] for c in range(0, S, CHUNK)], axis=1).astype(jnp.int32)


def _route(logits_t, B, S, cap):
    E = logits_t.shape[0]
    tri = jnp.asarray(np.triu(np.ones((SCAN, SCAN), np.float32), 1), BF16)
    spec = pl.BlockSpec((E, S), lambda b: (0, b))
    return pl.pallas_call(
        functools.partial(_route_body, cap),
        grid=(B,),
        in_specs=[spec, _const_spec(tri.shape)],
        out_specs=[spec, spec, pl.BlockSpec((1, E, S // CHUNK), lambda b: (b, 0, 0))],
        out_shape=[jax.ShapeDtypeStruct((E, B * S), jnp.int32), jax.ShapeDtypeStruct((E, B * S), F32),
                   jax.ShapeDtypeStruct((B, E, S // CHUNK), jnp.int32)],
        compiler_params=_params(("parallel",)),
        name="route",
    )(logits_t, tri)


def _window(starts_ref, first, kt, n_chunks, cap):
    start = starts_ref[first + kt]
    nxt = starts_ref[first + jnp.minimum(kt + 1, n_chunks - 1)]
    end = jnp.where(kt + 1 < n_chunks, nxt, cap)
    base = jnp.minimum(lax.shift_left(lax.shift_right_logical(start, 4), 4), cap - WINDOW_ROWS)
    return pl.multiple_of(base, 16), end <= base + WINDOW_ROWS


def _dispatch_body(cap, starts_ref, slot_ref, aff_ref, x_ref, xe_ref, tv_ref):
    b = pl.program_id(0)
    E, S = slot_ref.shape
    n_chunks = S // CHUNK
    windows = [[_window(starts_ref, (b * E + e) * n_chunks, kt, n_chunks, cap) for e in range(E)]
               for kt in range(n_chunks)]
    fits = functools.reduce(jnp.logical_and, [w[1] for per_chunk in windows for w in per_chunk])

    @pl.when(fits)
    def _():
        xe_ref[...] = jnp.zeros_like(xe_ref)
        tv_ref[...] = jnp.zeros_like(tv_ref)
        ri = lax.broadcasted_iota(jnp.int32, (WINDOW_ROWS, CHUNK), 0)
        for kt in range(n_chunks):
            cols = slice(kt * CHUNK, (kt + 1) * CHUNK)
            hits = [(ri + windows[kt][e][0]) == slot_ref[e:e + 1, cols] for e in range(E)]
            onehot = jnp.concatenate([jnp.where(h, 1.0, 0.0).astype(BF16) for h in hits], axis=0)
            rows_all = jnp.dot(onehot, x_ref[cols, :], preferred_element_type=F32)
            for e in range(E):
                rows = pl.ds(windows[kt][e][0], WINDOW_ROWS)
                xe_ref[0, e, rows, :] += rows_all[e * WINDOW_ROWS:(e + 1) * WINDOW_ROWS].astype(BF16)
                tv = jnp.sum(jnp.where(hits[e], aff_ref[e:e + 1, cols], 0.0), axis=1, keepdims=True)
                tv_ref[0, e, rows, :] += jnp.broadcast_to(tv, (WINDOW_ROWS, LANES))

    @pl.when(jnp.logical_not(fits))
    def _():
        ri = lax.broadcasted_iota(jnp.int32, (cap, S), 0)
        for e in range(E):
            hit = ri == slot_ref[e:e + 1, :]
            onehot = jnp.where(hit, 1.0, 0.0).astype(BF16)
            xe_ref[0, e] = jnp.dot(onehot, x_ref[...], preferred_element_type=F32).astype(BF16)
            tv = jnp.sum(jnp.where(hit, aff_ref[e:e + 1, :], 0.0), axis=1, keepdims=True)
            tv_ref[0, e] = jnp.broadcast_to(tv, (cap, LANES))


def _dispatch(starts, slot, aff, x1b, B, S, cap):
    E = slot.shape[0]
    D = x1b.shape[1]
    row = pl.BlockSpec((E, S), lambda b, st: (0, b))
    return pl.pallas_call(
        functools.partial(_dispatch_body, cap),
        grid_spec=pltpu.PrefetchScalarGridSpec(
            num_scalar_prefetch=1,
            grid=(B,),
            in_specs=[row, row, pl.BlockSpec((S, D), lambda b, st: (b, 0))],
            out_specs=[pl.BlockSpec((1, E, cap, D), lambda b, st: (b, 0, 0, 0)),
                       pl.BlockSpec((1, E, cap, LANES), lambda b, st: (b, 0, 0, 0))]),
        out_shape=[jax.ShapeDtypeStruct((B, E, cap, D), BF16),
                   jax.ShapeDtypeStruct((B, E, cap, LANES), F32)],
        compiler_params=_params(("arbitrary",)),
        name="dispatch",
    )(starts, slot, aff, x1b)


EXPERT_SEQS = 4


def _experts_body(xe_ref, tv_ref, wg_ref, wu_ref, wd_ref, y_ref):
    nb, _, cap, D = xe_ref.shape
    xe = xe_ref[...].reshape(nb * cap, D)
    tv = tv_ref[...].reshape(nb * cap, LANES)[:, 0:1]
    gate = jnp.dot(xe, wg_ref[0].astype(BF16), preferred_element_type=F32)
    up = jnp.dot(xe, wu_ref[0].astype(BF16), preferred_element_type=F32)
    h = (gate * jax.nn.sigmoid(gate) * up).astype(BF16)
    ye = jnp.dot(h, wd_ref[0].astype(BF16), preferred_element_type=F32) * tv
    y_ref[...] = ye.astype(BF16).reshape(nb, 1, cap, D)


def _experts(xe, tv, wg, wu, wd):
    B, E, cap, D = xe.shape
    F = wg.shape[2]
    nb = EXPERT_SEQS
    wspec = lambda shape: pl.BlockSpec(shape, lambda e, b: (e, 0, 0), pipeline_mode=pl.Buffered(1))
    tok = lambda width: pl.BlockSpec((nb, 1, cap, width), lambda e, b: (b, e, 0, 0))
    return pl.pallas_call(
        _experts_body,
        grid=(E, B // nb),
        in_specs=[tok(D), tok(LANES), wspec((1, D, F)), wspec((1, D, F)),
                  pl.BlockSpec((1, F, D), lambda e, b: (e, 0, 0))],
        out_specs=tok(D),
        out_shape=jax.ShapeDtypeStruct((B, E, cap, D), BF16),
        compiler_params=_params(("arbitrary", "arbitrary")),
        name="experts",
    )(xe, tv, wg, wu, wd)


def _combine_body(cap, tiles_per_seq, starts_ref, slot_ref, y_ref, x1_ref, g_ref, b_ref, o_ref):
    i = pl.program_id(0)
    E = slot_ref.shape[0]
    tm = slot_ref.shape[1]
    subs = tm // CHUNK
    n_chunks = tiles_per_seq * subs
    b = i // tiles_per_seq
    nt = (((0,), (0,)), ((), ()))
    windows = [[_window(starts_ref, (b * E + e) * n_chunks, (i % tiles_per_seq) * subs + sub, n_chunks, cap)
                for e in range(E)] for sub in range(subs)]
    fits = functools.reduce(jnp.logical_and, [w[1] for per_sub in windows for w in per_sub])

    def finish(sub, f):
        rows = pl.ds(sub * CHUNK, CHUNK)
        o_ref[rows, :] = _layer_norm(DEEPNORM_ALPHA * x1_ref[rows, :] + f, g_ref[...], b_ref[...])

    @pl.when(fits)
    def _():
        ri = lax.broadcasted_iota(jnp.int32, (WINDOW_ROWS, CHUNK), 0)
        fs = []
        for sub in range(subs):
            cols = slice(sub * CHUNK, (sub + 1) * CHUNK)
            onehot = jnp.concatenate(
                [jnp.where(ri + windows[sub][e][0] == slot_ref[e:e + 1, cols], 1.0, 0.0).astype(BF16)
                 for e in range(E)], axis=0)
            ywin = jnp.concatenate(
                [y_ref[0, e, pl.ds(windows[sub][e][0], WINDOW_ROWS), :] for e in range(E)], axis=0)
            fs.append(lax.dot_general(onehot, ywin, nt, preferred_element_type=F32))
        for sub in range(subs):
            finish(sub, fs[sub])

    @pl.when(jnp.logical_not(fits))
    def _():
        ri = lax.broadcasted_iota(jnp.int32, (cap, CHUNK), 0)
        yall = jnp.concatenate([y_ref[0, e] for e in range(E)], axis=0)
        for sub in range(subs):
            cols = slice(sub * CHUNK, (sub + 1) * CHUNK)
            onehot = jnp.concatenate(
                [jnp.where(ri == slot_ref[e:e + 1, cols], 1.0, 0.0).astype(BF16) for e in range(E)], axis=0)
            finish(sub, lax.dot_general(onehot, yall, nt, preferred_element_type=F32))


def _combine(starts, slot, y, x1, g, b, B, S, cap):
    T, D = x1.shape
    E = slot.shape[0]
    tm = TOKEN_TILE
    tiles_per_seq = S // tm
    const = lambda shape: pl.BlockSpec(shape, lambda i, st: (0,) * len(shape), pipeline_mode=pl.Buffered(1))
    return pl.pallas_call(
        functools.partial(_combine_body, cap, tiles_per_seq),
        grid_spec=pltpu.PrefetchScalarGridSpec(
            num_scalar_prefetch=1,
            grid=(T // tm,),
            in_specs=[pl.BlockSpec((E, tm), lambda i, st: (0, i)),
                      pl.BlockSpec((1, E, cap, D), lambda i, st: (i // tiles_per_seq, 0, 0, 0)),
                      pl.BlockSpec((tm, D), lambda i, st: (i, 0)),
                      const(g.shape), const(b.shape)],
            out_specs=pl.BlockSpec((tm, D), lambda i, st: (i, 0))),
        out_shape=jax.ShapeDtypeStruct((T, D), F32),
        compiler_params=_params(("arbitrary",)),
        name="combine",
    )(starts, slot, y, x1, g, b)


def _split_hi_lo(w):
    hi = w.astype(BF16)
    lo = (w - hi.astype(F32)).astype(BF16)
    return jnp.concatenate([hi, lo], axis=0)


def kernel(x, ln0_g, ln0_b, w_in, conv_w, attn_sink, w_attn_o, w_conv_o, w_out, ln1_g, ln1_b,
           w_router, w_gate, w_up, w_down, ln2_g, ln2_b):
    B, S, D = x.shape
    assert w_in.shape[0] == DEPTH == 1
    assert S % TOKEN_TILE == 0 and TOKEN_TILE % SUB == 0 and TOKEN_TILE % CHUNK == 0
    assert B % EXPERT_SEQS == 0
    T = B * S
    d_attn = N_HEADS * HEAD_DIM
    d_kv = N_KV_HEADS * HEAD_DIM
    cap = CAPACITY_FACTOR * S // N_EXPERTS
    row = lambda a: a.reshape(1, -1).astype(F32)

    splits = ((d_attn, HEAD_DIM ** -0.5 * LOG2E), (d_kv, 1.0), (d_kv, 1.0),
              (D, 1.0), (D, 1.0), (D, 1.0), (D, 1.0), (D, 1.0))
    xn, q, k, v, cb, cc, cx, ga, gc = _inproj(
        x.reshape(T, D), row(ln0_g), row(ln0_b), w_in[0].astype(BF16), splits)

    o = _attention(q, k, v, attn_sink[0].astype(F32) * LOG2E, B, S)

    x1, x1b, logits_t = _mixer_out(
        o, cb, cc, cx, ga, gc, xn, conv_w[0].astype(F32),
        w_attn_o[0].astype(BF16), w_conv_o[0].astype(BF16), w_out[0].astype(BF16),
        row(ln1_g[0]), row(ln1_b[0]), _split_hi_lo(w_router[0].T.astype(F32)), S)

    slot, aff, starts = _route(logits_t, B, S, cap)
    starts = starts.reshape(-1)

    xe, top_val = _dispatch(starts, slot, aff, x1b, B, S, cap)

    y = _experts(xe, top_val, w_gate[0], w_up[0], w_down[0])

    out = _combine(starts, slot, y, x1, row(ln2_g[0]), row(ln2_b[0]), B, S, cap)
    return out.reshape(B, S, D)
```

```python
import functools

import jax
import jax.numpy as jnp
import numpy as np
from jax import lax
from jax.experimental import pallas as pl
from jax.experimental.pallas import tpu as pltpu

F32 = jnp.float32
BF16 = jnp.bfloat16

N_HEADS = 16
N_KV_HEADS = 4
HEAD_DIM = 64
REP = N_HEADS // N_KV_HEADS
WINDOW = 128
BLOCK = 128
NEG_INF = -1e30
LOG2E = float(np.log2(np.e))
CONV_WIDTH = 3
N_EXPERTS = 16
CAPACITY_FACTOR = 2
DEPTH = 1
DEEPNORM_ALPHA = (2.0 * DEPTH) ** 0.25
LN_EPS = 1e-5

V7X_VMEM_BYTES = 64 * 1024 * 1024
LANES = 128
VMEM_LIMIT = 56 * 1024 * 1024

TOKEN_TILE = 512
SUB = BLOCK


def _layer_norm(x, g, b):
    mu = jnp.mean(x, axis=-1, keepdims=True)
    xc = x - mu
    var = jnp.mean(xc * xc, axis=-1, keepdims=True)
    return xc * lax.rsqrt(var + LN_EPS) * g + b


def _truncate_to_bf16(x):
    bits = lax.bitcast_convert_type(x, jnp.uint32) & jnp.uint32(0xFFFF0000)
    return lax.bitcast_convert_type(bits, F32)


def _params(sem):
    return pltpu.CompilerParams(dimension_semantics=sem, vmem_limit_bytes=VMEM_LIMIT)


def _const_spec(shape):
    zeros = (0,) * len(shape)
    return pl.BlockSpec(shape, lambda *_: zeros, pipeline_mode=pl.Buffered(1))


INPROJ_PARTS = 4


def _inproj_body(splits, x_ref, g_ref, b_ref, w_ref, xn_ref, *out_refs):
    pm = x_ref.shape[0] // INPROJ_PARTS
    groups = []
    for p in range(INPROJ_PARTS):
        r = pl.ds(p * pm, pm)
        xn = _layer_norm(x_ref[r, :], g_ref[...], b_ref[...])
        xn_ref[r, :] = xn
        groups.append((r, xn.astype(BF16)))
    for r, xb in groups:
        off = 0
        for (width, scale), o_ref in zip(splits, out_refs):
            chunk = min(width, 512)
            for c in range(0, width, chunk):
                acc = jnp.dot(xb, w_ref[:, off + c:off + c + chunk], preferred_element_type=F32)
                if scale != 1.0:
                    acc = acc * scale
                o_ref[r, c:c + chunk] = acc.astype(BF16)
            off += width


def _inproj(x2, g, b, w_bf16, splits):
    T, D = x2.shape
    n_in = w_bf16.shape[1]
    tm = TOKEN_TILE
    out_shape = [jax.ShapeDtypeStruct((T, D), F32)]
    out_specs = [pl.BlockSpec((tm, D), lambda i: (i, 0))]
    for width, _ in splits:
        out_shape.append(jax.ShapeDtypeStruct((T, width), BF16))
        out_specs.append(pl.BlockSpec((tm, width), lambda i: (i, 0)))
    return pl.pallas_call(
        functools.partial(_inproj_body, splits),
        grid=(T // tm,),
        in_specs=[
            pl.BlockSpec((tm, D), lambda i: (i, 0)),
            _const_spec((1, D)),
            _const_spec((1, D)),
            _const_spec((D, n_in)),
        ],
        out_specs=out_specs,
        out_shape=out_shape,
        compiler_params=_params(("parallel",)),
        name="inproj",
    )(x2, g, b, w_bf16)


def _alibi_slopes():
    return (2.0 ** (-8.0 * np.arange(1, N_HEADS + 1) / N_HEADS)).astype(np.float32)


def _attention_bias():
    slopes = _alibi_slopes().reshape(N_KV_HEADS, REP)
    i = np.arange(SUB)[None, :]
    j = np.arange(SUB)[:, None]
    out = np.empty((5, N_KV_HEADS, SUB, REP * SUB), np.float32)
    for var, (shift, masked) in enumerate([(-SUB, False), (-SUB, True), (0, False), (SUB, False), (SUB, True)]):
        dist = np.abs(i - (j + shift))
        valid = (dist <= WINDOW) & (not masked)
        for g in range(N_KV_HEADS):
            for r in range(REP):
                b = (-slopes[g, r] * dist.astype(np.float32)).astype(np.float64) * LOG2E
                out[var, g, :, r * SUB:(r + 1) * SUB] = np.where(valid, b, NEG_INF).astype(np.float32)
    return out


ONES_ROWS = 16


def _attention_body(n_sub, sink_ref, q_ref, kp_ref, kc_ref, kn_ref, vp_ref, vc_ref, vn_ref,
                    bias_ref, o_ref):
    n = pl.program_id(1)
    first = n == 0
    last = n == pl.num_programs(1) - 1
    kwin = jnp.concatenate([kp_ref[...], kc_ref[...], kn_ref[...]], axis=0)
    vwin = jnp.concatenate([vp_ref[...], vc_ref[...], vn_ref[...]], axis=0)
    vwin_t = vwin.astype(F32).T.astype(BF16)
    ones = jnp.ones((ONES_ROWS, vwin_t.shape[1]), BF16)

    def scores(j, g):
        qj = q_ref[j * SUB:(j + 1) * SUB, :]
        prev_var = jnp.where(first, 1, 0) if j == 0 else 0
        next_var = jnp.where(last, 4, 3) if j == n_sub - 1 else 3
        qg = jnp.concatenate(
            [qj[:, (g * REP + r) * HEAD_DIM:(g * REP + r + 1) * HEAD_DIM] for r in range(REP)], axis=0)
        kg = kwin[j * SUB:(j + 3) * SUB, g * HEAD_DIM:(g + 1) * HEAD_DIM]
        s = lax.dot_general(kg, qg, (((1,), (1,)), ((), ())), preferred_element_type=F32)
        bias = jnp.concatenate([bias_ref[prev_var, g], bias_ref[2, g], bias_ref[next_var, g]], axis=0)
        s = s + bias
        sink = jnp.concatenate(
            [jnp.full((1, SUB), sink_ref[g * REP + r], F32) for r in range(REP)], axis=1)
        m = jnp.maximum(jnp.max(s, axis=0, keepdims=True), sink)
        return s, m, sink

    def probs(s, m):
        return jnp.exp2(s - m).astype(BF16)

    def weighted_values(j, g, p, m, sink):
        vg_t = jnp.concatenate([vwin_t[g * HEAD_DIM:(g + 1) * HEAD_DIM], ones], axis=0)
        vg_t = vg_t[:, j * SUB:(j + 3) * SUB]
        ov = jnp.dot(vg_t, p, preferred_element_type=F32)
        denom = ov[HEAD_DIM:HEAD_DIM + 1] + jnp.exp2(sink - m)
        return ov[:HEAD_DIM] / denom

    units = [(j, g) for j in range(n_sub) for g in range(N_KV_HEADS)]
    stage1, stage2 = {}, {}
    heads_t = {j: [] for j in range(n_sub)}
    for step in range(len(units) + 2):
        if step >= 2:
            j, g = units[step - 2]
            p, m, sink = stage2.pop(step - 2)
            og_t = weighted_values(j, g, p, m, sink)
            heads_t[j] += [og_t[:, r * SUB:(r + 1) * SUB] for r in range(REP)]
            if g == N_KV_HEADS - 1:
                o_ref[j * SUB:(j + 1) * SUB, :] = jnp.concatenate(heads_t.pop(j), axis=0).T.astype(BF16)
        if 1 <= step <= len(units):
            s, m, sink = stage1.pop(step - 1)
            stage2[step - 1] = (probs(s, m), m, sink)
        if step < len(units):
            stage1[step] = scores(*units[step])


def _attention(q, k, v, sink, B, S):
    T, d_attn = q.shape
    d_kv = k.shape[1]
    tq = TOKEN_TILE
    n_sub = tq // SUB
    nq = S // tq
    nblk = S // SUB
    bias = jnp.asarray(_attention_bias())

    def q_map(b, n):
        return (b * nq + n, 0)

    def prev_map(b, n):
        return (b * nblk + jnp.maximum(n * n_sub - 1, 0), 0)

    def next_map(b, n):
        return (b * nblk + jnp.minimum((n + 1) * n_sub, nblk - 1), 0)

    kv_specs = [pl.BlockSpec((SUB, d_kv), prev_map), pl.BlockSpec((tq, d_kv), q_map),
                pl.BlockSpec((SUB, d_kv), next_map)]
    return pl.pallas_call(
        functools.partial(_attention_body, n_sub),
        grid=(B, nq),
        in_specs=[pl.BlockSpec(memory_space=pltpu.SMEM),
                  pl.BlockSpec((tq, d_attn), q_map)] + kv_specs + kv_specs + [_const_spec(bias.shape)],
        out_specs=pl.BlockSpec((tq, d_attn), q_map),
        out_shape=jax.ShapeDtypeStruct((T, d_attn), BF16),
        compiler_params=_params(("parallel", "parallel")),
        name="attention",
    )(sink, q, k, k, k, v, v, v, bias)


HALO = 16
MIXER_PARTS = 2


def _mixer_out_body(tiles_per_seq, o_ref, cb_ref, cc_ref, cx_ref, ccp_ref, cxp_ref, ccn_ref, cxn_ref,
                    ga_ref, gc_ref, xn_ref, cw_ref, wa_ref, wc_ref, wo_ref, g_ref, b_ref, wr_ref,
                    x1_ref, x1b_ref, lg_ref):
    i = pl.program_id(0)
    n = i % tiles_per_seq
    tm = cc_ref.shape[0]
    pm = tm // MIXER_PARTS
    u = cc_ref[...].astype(F32) * cx_ref[...].astype(F32)
    u_before = ccp_ref[HALO - 1:HALO, :].astype(F32) * cxp_ref[HALO - 1:HALO, :].astype(F32)
    u_after = ccn_ref[0:1, :].astype(F32) * cxn_ref[0:1, :].astype(F32)
    u_before = jnp.where(n == 0, 0.0, u_before)
    u_after = jnp.where(n == tiles_per_seq - 1, 0.0, u_after)
    cw = cw_ref[...]
    row = lax.broadcasted_iota(jnp.int32, (pm, u.shape[1]), 0)

    def project_in(p):
        r = pl.ds(p * pm, pm)
        y_attn = jnp.dot(o_ref[r, :], wa_ref[...], preferred_element_type=F32)
        up = u[p * pm:(p + 1) * pm]
        before = u_before if p == 0 else u[p * pm - 1:p * pm]
        after = u_after if p == MIXER_PARTS - 1 else u[(p + 1) * pm:(p + 1) * pm + 1]
        um1 = jnp.where(row == 0, before, pltpu.roll(up, 1, 0))
        up1 = jnp.where(row == pm - 1, after, pltpu.roll(up, pm - 1, 0))
        y = cb_ref[r, :].astype(F32) * (cw[0:1, :] * um1 + cw[1:2, :] * up + cw[2:3, :] * up1)
        y_conv = jnp.dot(y.astype(BF16), wc_ref[...], preferred_element_type=F32)
        return y_attn, y_conv

    def project_out(p, y_attn, y_conv):
        r = pl.ds(p * pm, pm)
        merged = (jax.nn.sigmoid(ga_ref[r, :].astype(F32)) * y_attn
                  + jax.nn.sigmoid(gc_ref[r, :].astype(F32)) * y_conv)
        return jnp.dot(merged.astype(BF16), wo_ref[...], preferred_element_type=F32)

    def finish(p, h):
        r = pl.ds(p * pm, pm)
        x1 = _layer_norm(DEEPNORM_ALPHA * xn_ref[r, :] + h, g_ref[...], b_ref[...])
        x1_ref[r, :] = x1
        x1b_ref[r, :] = x1.astype(BF16)
        x1_top = _truncate_to_bf16(x1)
        x1_hi = x1_top.astype(BF16)
        x1_lo = (x1 - x1_top).astype(BF16)
        nt = (((1,), (1,)), ((), ()))
        E = lg_ref.shape[0]
        by_hi = lax.dot_general(wr_ref[...], x1_hi, nt, preferred_element_type=F32)
        by_lo = lax.dot_general(wr_ref[0:E, :], x1_lo, nt, preferred_element_type=F32)
        lg_ref[:, r] = by_hi[0:E] + (by_hi[E:2 * E] + by_lo)

    stage1, stage2 = {}, {}
    for step in range(MIXER_PARTS + 2):
        if step < MIXER_PARTS:
            stage1[step] = project_in(step)
        if 0 <= step - 1 < MIXER_PARTS:
            stage2[step - 1] = project_out(step - 1, *stage1.pop(step - 1))
        if 0 <= step - 2 < MIXER_PARTS:
            finish(step - 2, stage2.pop(step - 2))


def _mixer_out(o, cb, cc, cx, ga, gc, xn, conv_w, wa, wc, wo, g, b, wr_t, S):
    T, D = xn.shape
    tm = TOKEN_TILE
    tiles_per_seq = S // tm
    hb = tm // HALO
    n_halo = T // HALO
    E = wr_t.shape[0] // 2

    def tile(i):
        return (i, 0)

    def before(i):
        return (jnp.maximum(i * hb - 1, 0), 0)

    def after(i):
        return (jnp.minimum((i + 1) * hb, n_halo - 1), 0)

    tspec = pl.BlockSpec((tm, D), tile)
    return pl.pallas_call(
        functools.partial(_mixer_out_body, tiles_per_seq),
        grid=(T // tm,),
        in_specs=[tspec, tspec, tspec, tspec,
                  pl.BlockSpec((HALO, D), before), pl.BlockSpec((HALO, D), before),
                  pl.BlockSpec((HALO, D), after), pl.BlockSpec((HALO, D), after),
                  tspec, tspec, tspec,
                  _const_spec(conv_w.shape), _const_spec(wa.shape), _const_spec(wc.shape),
                  _const_spec(wo.shape), _const_spec(g.shape), _const_spec(b.shape),
                  _const_spec(wr_t.shape)],
        out_specs=[tspec, tspec, pl.BlockSpec((E, tm), lambda i: (0, i))],
        out_shape=[jax.ShapeDtypeStruct((T, D), F32), jax.ShapeDtypeStruct((T, D), BF16),
                   jax.ShapeDtypeStruct((E, T), F32)],
        compiler_params=_params(("parallel",)),
        name="mixer_out",
    )(o, cb, cc, cx, cc, cx, cc, cx, ga, gc, xn, conv_w, wa, wc, wo, g, b, wr_t)


VALUE_BITS = 32
RADIX_BITS = 3
SCAN = 256
CHUNK = 256
WINDOW_ROWS = 80


def _exclusive_count(mask, tri):
    E, S = mask.shape
    mb = mask.astype(BF16)
    carry = jnp.zeros((E, 1), F32)
    parts = []
    for c in range(0, S, SCAN):
        chunk = mb[:, c:c + SCAN]
        parts.append(jnp.dot(chunk, tri, preferred_element_type=F32) + carry)
        carry = carry + jnp.sum(chunk.astype(F32), axis=1, keepdims=True)
    return jnp.concatenate(parts, axis=1)


def _route_body(cap, lg_ref, tri_ref, slot_ref, aff_ref, start_ref):
    logits = lg_ref[...]
    mx = jnp.max(logits, axis=0, keepdims=True)
    ex = jnp.exp(logits - mx)
    aff = ex / jnp.sum(ex, axis=0, keepdims=True)
    aff_ref[...] = aff
    bits = pltpu.bitcast(aff, jnp.int32)
    E = bits.shape[0]

    def count_ge(cand):
        return jnp.sum((bits >= cand).astype(F32), axis=1, keepdims=True)

    def step(k, prefix):
        shift = VALUE_BITS - 1 - RADIX_BITS * (k + 1)
        out = prefix
        for digit in range(1, 2 ** RADIX_BITS):
            cand = prefix | lax.shift_left(jnp.int32(digit), shift)
            out = jnp.where(count_ge(cand) >= cap, cand, out)
        return out

    top = jnp.full((E, 1), 1 << (VALUE_BITS - 2), jnp.int32)
    prefix = jnp.where(count_ge(top) >= cap, top, 0)
    thresh = lax.fori_loop(0, (VALUE_BITS - 2) // RADIX_BITS, step, prefix)
    gt = bits > thresh
    eq = bits == thresh
    need = cap - jnp.sum(gt.astype(F32), axis=1, keepdims=True)
    tri = tri_ref[...]
    sel = gt | (eq & (_exclusive_count(eq, tri) < need))
    pos = _exclusive_count(sel, tri)
    slot_ref[...] = jnp.where(sel, pos, -1.0).astype(jnp.int32)
    S = pos.shape[1]
    start_ref[0] = jnp.concatenate([pos[:, c:c + 1] for c in range(0, S, CHUNK)], axis=1).astype(jnp.int32)


def _route(logits_t, B, S, cap):
    E = logits_t.shape[0]
    tri = jnp.asarray(np.triu(np.ones((SCAN, SCAN), np.float32), 1), BF16)
    spec = pl.BlockSpec((E, S), lambda b: (0, b))
    return pl.pallas_call(
        functools.partial(_route_body, cap),
        grid=(B,),
        in_specs=[spec, _const_spec(tri.shape)],
        out_specs=[spec, spec, pl.BlockSpec((1, E, S // CHUNK), lambda b: (b, 0, 0))],
        out_shape=[jax.ShapeDtypeStruct((E, B * S), jnp.int32), jax.ShapeDtypeStruct((E, B * S), F32),
                   jax.ShapeDtypeStruct((B, E, S // CHUNK), jnp.int32)],
        compiler_params=_params(("parallel",)),
        name="route",
    )(logits_t, tri)


def _window(starts_ref, first, kt, n_chunks, cap):
    start = starts_ref[first + kt]
    nxt = starts_ref[first + jnp.minimum(kt + 1, n_chunks - 1)]
    end = jnp.where(kt + 1 < n_chunks, nxt, cap)
    base = jnp.minimum(lax.shift_left(lax.shift_right_logical(start, 4), 4), cap - WINDOW_ROWS)
    return pl.multiple_of(base, 16), end <= base + WINDOW_ROWS


def _dispatch_body(cap, starts_ref, slot_ref, aff_ref, x_ref, xe_ref, tv_ref):
    b = pl.program_id(0)
    E, S = slot_ref.shape
    n_chunks = S // CHUNK
    windows = [[_window(starts_ref, (b * E + e) * n_chunks, kt, n_chunks, cap) for e in range(E)]
               for kt in range(n_chunks)]
    fits = functools.reduce(jnp.logical_and, [w[1] for per_chunk in windows for w in per_chunk])

    @pl.when(fits)
    def _():
        xe_ref[...] = jnp.zeros_like(xe_ref)
        tv_ref[...] = jnp.zeros_like(tv_ref)
        ri = lax.broadcasted_iota(jnp.int32, (WINDOW_ROWS, CHUNK), 0)
        for kt in range(n_chunks):
            cols = slice(kt * CHUNK, (kt + 1) * CHUNK)
            hits = [(ri + windows[kt][e][0]) == slot_ref[e:e + 1, cols] for e in range(E)]
            onehot = jnp.concatenate([jnp.where(h, 1.0, 0.0).astype(BF16) for h in hits], axis=0)
            rows_all = jnp.dot(onehot, x_ref[cols, :], preferred_element_type=F32)
            for e in range(E):
                rows = pl.ds(windows[kt][e][0], WINDOW_ROWS)
                xe_ref[0, e, rows, :] += rows_all[e * WINDOW_ROWS:(e + 1) * WINDOW_ROWS].astype(BF16)
                tv = jnp.sum(jnp.where(hits[e], aff_ref[e:e + 1, cols], 0.0), axis=1, keepdims=True)
                tv_ref[0, e, rows, :] += jnp.broadcast_to(tv, (WINDOW_ROWS, LANES))

    @pl.when(jnp.logical_not(fits))
    def _():
        ri = lax.broadcasted_iota(jnp.int32, (cap, S), 0)
        for e in range(E):
            hit = ri == slot_ref[e:e + 1, :]
            onehot = jnp.where(hit, 1.0, 0.0).astype(BF16)
            xe_ref[0, e] = jnp.dot(onehot, x_ref[...], preferred_element_type=F32).astype(BF16)
            tv = jnp.sum(jnp.where(hit, aff_ref[e:e + 1, :], 0.0), axis=1, keepdims=True)
            tv_ref[0, e] = jnp.broadcast_to(tv, (cap, LANES))


def _dispatch(starts, slot, aff, x1b, B, S, cap):
    E = slot.shape[0]
    D = x1b.shape[1]
    row = pl.BlockSpec((E, S), lambda b, st: (0, b))
    return pl.pallas_call(
        functools.partial(_dispatch_body, cap),
        grid_spec=pltpu.PrefetchScalarGridSpec(
            num_scalar_prefetch=1,
            grid=(B,),
            in_specs=[row, row, pl.BlockSpec((S, D), lambda b, st: (b, 0))],
            out_specs=[pl.BlockSpec((1, E, cap, D), lambda b, st: (b, 0, 0, 0)),
                       pl.BlockSpec((1, E, cap, LANES), lambda b, st: (b, 0, 0, 0))]),
        out_shape=[jax.ShapeDtypeStruct((B, E, cap, D), BF16),
                   jax.ShapeDtypeStruct((B, E, cap, LANES), F32)],
        compiler_params=_params(("arbitrary",)),
        name="dispatch",
    )(starts, slot, aff, x1b)


EXPERT_SEQS = 4


def _experts_body(xe_ref, tv_ref, wg_ref, wu_ref, wd_ref, y_ref):
    nb, _, cap, D = xe_ref.shape
    xe = xe_ref[...].reshape(nb * cap, D)
    tv = tv_ref[...].reshape(nb * cap, LANES)[:, 0:1]
    gate = jnp.dot(xe, wg_ref[0].astype(BF16), preferred_element_type=F32)
    up = jnp.dot(xe, wu_ref[0].astype(BF16), preferred_element_type=F32)
    h = (gate * jax.nn.sigmoid(gate) * up).astype(BF16)
    ye = jnp.dot(h, wd_ref[0].astype(BF16), preferred_element_type=F32) * tv
    y_ref[...] = ye.astype(BF16).reshape(nb, 1, cap, D)


def _experts(xe, tv, wg, wu, wd):
    B, E, cap, D = xe.shape
    F = wg.shape[2]
    nb = EXPERT_SEQS
    wspec = lambda shape: pl.BlockSpec(shape, lambda e, b: (e, 0, 0), pipeline_mode=pl.Buffered(1))
    tok = lambda width: pl.BlockSpec((nb, 1, cap, width), lambda e, b: (b, e, 0, 0))
    return pl.pallas_call(
        _experts_body,
        grid=(E, B // nb),
        in_specs=[tok(D), tok(LANES), wspec((1, D, F)), wspec((1, D, F)),
                  pl.BlockSpec((1, F, D), lambda e, b: (e, 0, 0))],
        out_specs=tok(D),
        out_shape=jax.ShapeDtypeStruct((B, E, cap, D), BF16),
        compiler_params=_params(("arbitrary", "arbitrary")),
        name="experts",
    )(xe, tv, wg, wu, wd)


def _combine_body(cap, tiles_per_seq, starts_ref, slot_ref, y_ref, x1_ref, g_ref, b_ref, o_ref):
    i = pl.program_id(0)
    E = slot_ref.shape[0]
    tm = slot_ref.shape[1]
    subs = tm // CHUNK
    n_chunks = tiles_per_seq * subs
    b = i // tiles_per_seq
    nt = (((0,), (0,)), ((), ()))
    windows = [[_window(starts_ref, (b * E + e) * n_chunks, (i % tiles_per_seq) * subs + sub, n_chunks, cap)
                for e in range(E)] for sub in range(subs)]
    fits = functools.reduce(jnp.logical_and, [w[1] for per_sub in windows for w in per_sub])

    def finish(sub, f):
        rows = pl.ds(sub * CHUNK, CHUNK)
        o_ref[rows, :] = _layer_norm(DEEPNORM_ALPHA * x1_ref[rows, :] + f, g_ref[...], b_ref[...])

    @pl.when(fits)
    def _():
        ri = lax.broadcasted_iota(jnp.int32, (WINDOW_ROWS, CHUNK), 0)
        fs = []
        for sub in range(subs):
            cols = slice(sub * CHUNK, (sub + 1) * CHUNK)
            onehot = jnp.concatenate(
                [jnp.where(ri + windows[sub][e][0] == slot_ref[e:e + 1, cols], 1.0, 0.0).astype(BF16)
                 for e in range(E)], axis=0)
            ywin = jnp.concatenate(
                [y_ref[0, e, pl.ds(windows[sub][e][0], WINDOW_ROWS), :] for e in range(E)], axis=0)
            fs.append(lax.dot_general(onehot, ywin, nt, preferred_element_type=F32))
        for sub in range(subs):
            finish(sub, fs[sub])

    @pl.when(jnp.logical_not(fits))
    def _():
        ri = lax.broadcasted_iota(jnp.int32, (cap, CHUNK), 0)
        yall = jnp.concatenate([y_ref[0, e] for e in range(E)], axis=0)
        for sub in range(subs):
            cols = slice(sub * CHUNK, (sub + 1) * CHUNK)
            onehot = jnp.concatenate(
                [jnp.where(ri == slot_ref[e:e + 1, cols], 1.0, 0.0).astype(BF16) for e in range(E)], axis=0)
            finish(sub, lax.dot_general(onehot, yall, nt, preferred_element_type=F32))


def _combine(starts, slot, y, x1, g, b, B, S, cap):
    T, D = x1.shape
    E = slot.shape[0]
    tm = TOKEN_TILE
    tiles_per_seq = S // tm
    const = lambda shape: pl.BlockSpec(shape, lambda i, st: (0,) * len(shape), pipeline_mode=pl.Buffered(1))
    return pl.pallas_call(
        functools.partial(_combine_body, cap, tiles_per_seq),
        grid_spec=pltpu.PrefetchScalarGridSpec(
            num_scalar_prefetch=1,
            grid=(T // tm,),
            in_specs=[pl.BlockSpec((E, tm), lambda i, st: (0, i)),
                      pl.BlockSpec((1, E, cap, D), lambda i, st: (i // tiles_per_seq, 0, 0, 0)),
                      pl.BlockSpec((tm, D), lambda i, st: (i, 0)),
                      const(g.shape), const(b.shape)],
            out_specs=pl.BlockSpec((tm, D), lambda i, st: (i, 0))),
        out_shape=jax.ShapeDtypeStruct((T, D), F32),
        compiler_params=_params(("arbitrary",)),
        name="combine",
    )(starts, slot, y, x1, g, b)


def _split_hi_lo(w):
    hi = _truncate_to_bf16(w)
    return jnp.concatenate([hi.astype(BF16), (w - hi).astype(BF16)], axis=0)


def kernel(x, ln0_g, ln0_b, w_in, conv_w, attn_sink, w_attn_o, w_conv_o, w_out, ln1_g, ln1_b,
           w_router, w_gate, w_up, w_down, ln2_g, ln2_b):
    B, S, D = x.shape
    assert w_in.shape[0] == DEPTH == 1
    assert S % TOKEN_TILE == 0 and TOKEN_TILE % SUB == 0 and TOKEN_TILE % CHUNK == 0
    assert B % EXPERT_SEQS == 0
    T = B * S
    d_attn = N_HEADS * HEAD_DIM
    d_kv = N_KV_HEADS * HEAD_DIM
    cap = CAPACITY_FACTOR * S // N_EXPERTS
    row = lambda a: a.reshape(1, -1).astype(F32)

    splits = ((d_attn, HEAD_DIM ** -0.5 * LOG2E), (d_kv, 1.0), (d_kv, 1.0),
              (D, 1.0), (D, 1.0), (D, 1.0), (D, 1.0), (D, 1.0))
    xn, q, k, v, cb, cc, cx, ga, gc = _inproj(
        x.reshape(T, D), row(ln0_g), row(ln0_b), w_in[0].astype(BF16), splits)

    o = _attention(q, k, v, attn_sink[0].astype(F32) * LOG2E, B, S)

    x1, x1b, logits_t = _mixer_out(
        o, cb, cc, cx, ga, gc, xn, conv_w[0].astype(F32),
        w_attn_o[0].astype(BF16), w_conv_o[0].astype(BF16), w_out[0].astype(BF16),
        row(ln1_g[0]), row(ln1_b[0]), _split_hi_lo(w_router[0].T.astype(F32)), S)

    slot, aff, starts = _route(logits_t, B, S, cap)
    starts = starts.reshape(-1)

    xe, top_val = _dispatch(starts, slot, aff, x1b, B, S, cap)

    y = _experts(xe, top_val, w_gate[0], w_up[0], w_down[0])

    out = _combine(starts, slot, y, x1, row(ln2_g[0]), row(ln2_b[0]), B, S, cap)
    return out.reshape(B, S, D)
```

```python
import functools

import jax
import jax.numpy as jnp
import numpy as np
from jax import lax
from jax.experimental import pallas as pl
from jax.experimental.pallas import tpu as pltpu

F32 = jnp.float32
BF16 = jnp.bfloat16

N_HEADS = 16
N_KV_HEADS = 4
HEAD_DIM = 64
REP = N_HEADS // N_KV_HEADS
WINDOW = 128
BLOCK = 128
NEG_INF = -1e30
LOG2E = float(np.log2(np.e))
CONV_WIDTH = 3
N_EXPERTS = 16
CAPACITY_FACTOR = 2
DEPTH = 1
DEEPNORM_ALPHA = (2.0 * DEPTH) ** 0.25
LN_EPS = 1e-5

V7X_VMEM_BYTES = 64 * 1024 * 1024
LANES = 128
VMEM_LIMIT = 56 * 1024 * 1024

TOKEN_TILE = 512
WIDE_TILE = 1024
SUB = BLOCK


def _layer_norm(x, g, b):
    mu = jnp.mean(x, axis=-1, keepdims=True)
    xc = x - mu
    var = jnp.mean(xc * xc, axis=-1, keepdims=True)
    return xc * lax.rsqrt(var + LN_EPS) * g + b


def _truncate_to_bf16(x):
    bits = lax.bitcast_convert_type(x, jnp.uint32) & jnp.uint32(0xFFFF0000)
    return lax.bitcast_convert_type(bits, F32)


def _params(sem):
    return pltpu.CompilerParams(dimension_semantics=sem, vmem_limit_bytes=VMEM_LIMIT)


def _const_spec(shape):
    zeros = (0,) * len(shape)
    return pl.BlockSpec(shape, lambda *_: zeros, pipeline_mode=pl.Buffered(1))


INPROJ_PARTS = 4


def _inproj_body(splits, x_ref, g_ref, b_ref, w_ref, xn_ref, *out_refs):
    pm = x_ref.shape[0] // INPROJ_PARTS
    groups = []
    for p in range(INPROJ_PARTS):
        r = pl.ds(p * pm, pm)
        xn = _layer_norm(x_ref[r, :], g_ref[...], b_ref[...])
        xn_ref[r, :] = xn
        groups.append((r, xn.astype(BF16)))
    for r, xb in groups:
        off = 0
        for (width, scale), o_ref in zip(splits, out_refs):
            chunk = min(width, 512)
            for c in range(0, width, chunk):
                acc = jnp.dot(xb, w_ref[:, off + c:off + c + chunk], preferred_element_type=F32)
                if scale != 1.0:
                    acc = acc * scale
                o_ref[r, c:c + chunk] = acc.astype(BF16)
            off += width


def _inproj(x2, g, b, w_bf16, splits):
    T, D = x2.shape
    n_in = w_bf16.shape[1]
    tm = TOKEN_TILE
    out_shape = [jax.ShapeDtypeStruct((T, D), F32)]
    out_specs = [pl.BlockSpec((tm, D), lambda i: (i, 0))]
    for width, _ in splits:
        out_shape.append(jax.ShapeDtypeStruct((T, width), BF16))
        out_specs.append(pl.BlockSpec((tm, width), lambda i: (i, 0)))
    return pl.pallas_call(
        functools.partial(_inproj_body, splits),
        grid=(T // tm,),
        in_specs=[
            pl.BlockSpec((tm, D), lambda i: (i, 0)),
            _const_spec((1, D)),
            _const_spec((1, D)),
            _const_spec((D, n_in)),
        ],
        out_specs=out_specs,
        out_shape=out_shape,
        compiler_params=_params(("parallel",)),
        name="inproj",
    )(x2, g, b, w_bf16)


def _alibi_slopes():
    return (2.0 ** (-8.0 * np.arange(1, N_HEADS + 1) / N_HEADS)).astype(np.float32)


def _attention_bias():
    slopes = _alibi_slopes().reshape(N_KV_HEADS, REP)
    i = np.arange(SUB)[None, :]
    j = np.arange(SUB)[:, None]
    out = np.empty((5, N_KV_HEADS, SUB, REP * SUB), np.float32)
    for var, (shift, masked) in enumerate([(-SUB, False), (-SUB, True), (0, False), (SUB, False), (SUB, True)]):
        dist = np.abs(i - (j + shift))
        valid = (dist <= WINDOW) & (not masked)
        for g in range(N_KV_HEADS):
            for r in range(REP):
                b = (-slopes[g, r] * dist.astype(np.float32)).astype(np.float64) * LOG2E
                out[var, g, :, r * SUB:(r + 1) * SUB] = np.where(valid, b, NEG_INF).astype(np.float32)
    return out


ONES_ROWS = 16


def _attention_body(n_sub, sink_ref, q_ref, kp_ref, kc_ref, kn_ref, vp_ref, vc_ref, vn_ref,
                    bias_ref, o_ref):
    n = pl.program_id(1)
    first = n == 0
    last = n == pl.num_programs(1) - 1
    kwin = jnp.concatenate([kp_ref[...], kc_ref[...], kn_ref[...]], axis=0)
    vwin = jnp.concatenate([vp_ref[...], vc_ref[...], vn_ref[...]], axis=0)
    vwin_t = vwin.astype(F32).T.astype(BF16)
    ones = jnp.ones((ONES_ROWS, vwin_t.shape[1]), BF16)

    def scores(j, g):
        qj = q_ref[j * SUB:(j + 1) * SUB, :]
        prev_var = jnp.where(first, 1, 0) if j == 0 else 0
        next_var = jnp.where(last, 4, 3) if j == n_sub - 1 else 3
        qg = jnp.concatenate(
            [qj[:, (g * REP + r) * HEAD_DIM:(g * REP + r + 1) * HEAD_DIM] for r in range(REP)], axis=0)
        kg = kwin[j * SUB:(j + 3) * SUB, g * HEAD_DIM:(g + 1) * HEAD_DIM]
        s = lax.dot_general(kg, qg, (((1,), (1,)), ((), ())), preferred_element_type=F32)
        bias = jnp.concatenate([bias_ref[prev_var, g], bias_ref[2, g], bias_ref[next_var, g]], axis=0)
        s = s + bias
        sink = jnp.concatenate(
            [jnp.full((1, SUB), sink_ref[g * REP + r], F32) for r in range(REP)], axis=1)
        m = jnp.maximum(jnp.max(s, axis=0, keepdims=True), sink)
        return s, m, sink

    def probs(s, m):
        return jnp.exp2(s - m).astype(BF16)

    def weighted_values(j, g, p, m, sink):
        vg_t = jnp.concatenate([vwin_t[g * HEAD_DIM:(g + 1) * HEAD_DIM], ones], axis=0)
        vg_t = vg_t[:, j * SUB:(j + 3) * SUB]
        ov = jnp.dot(vg_t, p, preferred_element_type=F32)
        denom = ov[HEAD_DIM:HEAD_DIM + 1] + jnp.exp2(sink - m)
        return ov[:HEAD_DIM] / denom

    units = [(j, g) for j in range(n_sub) for g in range(N_KV_HEADS)]
    stage1, stage2 = {}, {}
    heads_t = {j: [] for j in range(n_sub)}
    for step in range(len(units) + 2):
        if step >= 2:
            j, g = units[step - 2]
            p, m, sink = stage2.pop(step - 2)
            og_t = weighted_values(j, g, p, m, sink)
            heads_t[j] += [og_t[:, r * SUB:(r + 1) * SUB] for r in range(REP)]
            if g == N_KV_HEADS - 1:
                o_ref[j * SUB:(j + 1) * SUB, :] = jnp.concatenate(heads_t.pop(j), axis=0).T.astype(BF16)
        if 1 <= step <= len(units):
            s, m, sink = stage1.pop(step - 1)
            stage2[step - 1] = (probs(s, m), m, sink)
        if step < len(units):
            stage1[step] = scores(*units[step])


def _attention(q, k, v, sink, B, S):
    T, d_attn = q.shape
    d_kv = k.shape[1]
    tq = WIDE_TILE
    n_sub = tq // SUB
    nq = S // tq
    nblk = S // SUB
    bias = jnp.asarray(_attention_bias())

    def q_map(b, n):
        return (b * nq + n, 0)

    def prev_map(b, n):
        return (b * nblk + jnp.maximum(n * n_sub - 1, 0), 0)

    def next_map(b, n):
        return (b * nblk + jnp.minimum((n + 1) * n_sub, nblk - 1), 0)

    kv_specs = [pl.BlockSpec((SUB, d_kv), prev_map), pl.BlockSpec((tq, d_kv), q_map),
                pl.BlockSpec((SUB, d_kv), next_map)]
    return pl.pallas_call(
        functools.partial(_attention_body, n_sub),
        grid=(B, nq),
        in_specs=[pl.BlockSpec(memory_space=pltpu.SMEM),
                  pl.BlockSpec((tq, d_attn), q_map)] + kv_specs + kv_specs + [_const_spec(bias.shape)],
        out_specs=pl.BlockSpec((tq, d_attn), q_map),
        out_shape=jax.ShapeDtypeStruct((T, d_attn), BF16),
        compiler_params=_params(("parallel", "parallel")),
        name="attention",
    )(sink, q, k, k, k, v, v, v, bias)


HALO = 16
MIXER_PARTS = 2


def _mixer_out_body(tiles_per_seq, o_ref, cb_ref, cc_ref, cx_ref, ccp_ref, cxp_ref, ccn_ref, cxn_ref,
                    ga_ref, gc_ref, xn_ref, cw_ref, wa_ref, wc_ref, wo_ref, g_ref, b_ref, wr_ref,
                    x1_ref, x1b_ref, lg_ref):
    i = pl.program_id(0)
    n = i % tiles_per_seq
    tm = cc_ref.shape[0]
    pm = tm // MIXER_PARTS
    u = cc_ref[...].astype(F32) * cx_ref[...].astype(F32)
    u_before = ccp_ref[HALO - 1:HALO, :].astype(F32) * cxp_ref[HALO - 1:HALO, :].astype(F32)
    u_after = ccn_ref[0:1, :].astype(F32) * cxn_ref[0:1, :].astype(F32)
    u_before = jnp.where(n == 0, 0.0, u_before)
    u_after = jnp.where(n == tiles_per_seq - 1, 0.0, u_after)
    cw = cw_ref[...]
    row = lax.broadcasted_iota(jnp.int32, (pm, u.shape[1]), 0)

    def project_in(p):
        r = pl.ds(p * pm, pm)
        y_attn = jnp.dot(o_ref[r, :], wa_ref[...], preferred_element_type=F32)
        up = u[p * pm:(p + 1) * pm]
        before = u_before if p == 0 else u[p * pm - 1:p * pm]
        after = u_after if p == MIXER_PARTS - 1 else u[(p + 1) * pm:(p + 1) * pm + 1]
        um1 = jnp.where(row == 0, before, pltpu.roll(up, 1, 0))
        up1 = jnp.where(row == pm - 1, after, pltpu.roll(up, pm - 1, 0))
        y = cb_ref[r, :].astype(F32) * (cw[0:1, :] * um1 + cw[1:2, :] * up + cw[2:3, :] * up1)
        y_conv = jnp.dot(y.astype(BF16), wc_ref[...], preferred_element_type=F32)
        return y_attn, y_conv

    def project_out(p, y_attn, y_conv):
        r = pl.ds(p * pm, pm)
        merged = (jax.nn.sigmoid(ga_ref[r, :].astype(F32)) * y_attn
                  + jax.nn.sigmoid(gc_ref[r, :].astype(F32)) * y_conv)
        return jnp.dot(merged.astype(BF16), wo_ref[...], preferred_element_type=F32)

    def finish(p, h):
        r = pl.ds(p * pm, pm)
        x1 = _layer_norm(DEEPNORM_ALPHA * xn_ref[r, :] + h, g_ref[...], b_ref[...])
        x1_ref[r, :] = x1
        x1b_ref[r, :] = x1.astype(BF16)
        x1_top = _truncate_to_bf16(x1)
        x1_hi = x1_top.astype(BF16)
        x1_lo = (x1 - x1_top).astype(BF16)
        nt = (((1,), (1,)), ((), ()))
        E = lg_ref.shape[0]
        by_hi = lax.dot_general(wr_ref[...], x1_hi, nt, preferred_element_type=F32)
        by_lo = lax.dot_general(wr_ref[0:E, :], x1_lo, nt, preferred_element_type=F32)
        lg_ref[:, r] = by_hi[0:E] + (by_hi[E:2 * E] + by_lo)

    stage1, stage2 = {}, {}
    for step in range(MIXER_PARTS + 2):
        if step < MIXER_PARTS:
            stage1[step] = project_in(step)
        if 0 <= step - 1 < MIXER_PARTS:
            stage2[step - 1] = project_out(step - 1, *stage1.pop(step - 1))
        if 0 <= step - 2 < MIXER_PARTS:
            finish(step - 2, stage2.pop(step - 2))


def _mixer_out(o, cb, cc, cx, ga, gc, xn, conv_w, wa, wc, wo, g, b, wr_t, S):
    T, D = xn.shape
    tm = TOKEN_TILE
    tiles_per_seq = S // tm
    hb = tm // HALO
    n_halo = T // HALO
    E = wr_t.shape[0] // 2

    def tile(i):
        return (i, 0)

    def before(i):
        return (jnp.maximum(i * hb - 1, 0), 0)

    def after(i):
        return (jnp.minimum((i + 1) * hb, n_halo - 1), 0)

    tspec = pl.BlockSpec((tm, D), tile)
    return pl.pallas_call(
        functools.partial(_mixer_out_body, tiles_per_seq),
        grid=(T // tm,),
        in_specs=[tspec, tspec, tspec, tspec,
                  pl.BlockSpec((HALO, D), before), pl.BlockSpec((HALO, D), before),
                  pl.BlockSpec((HALO, D), after), pl.BlockSpec((HALO, D), after),
                  tspec, tspec, tspec,
                  _const_spec(conv_w.shape), _const_spec(wa.shape), _const_spec(wc.shape),
                  _const_spec(wo.shape), _const_spec(g.shape), _const_spec(b.shape),
                  _const_spec(wr_t.shape)],
        out_specs=[tspec, tspec, pl.BlockSpec((E, tm), lambda i: (0, i))],
        out_shape=[jax.ShapeDtypeStruct((T, D), F32), jax.ShapeDtypeStruct((T, D), BF16),
                   jax.ShapeDtypeStruct((E, T), F32)],
        compiler_params=_params(("parallel",)),
        name="mixer_out",
    )(o, cb, cc, cx, cc, cx, cc, cx, ga, gc, xn, conv_w, wa, wc, wo, g, b, wr_t)


VALUE_BITS = 32
RADIX_BITS = 3
SCAN = 256
CHUNK = 256
WINDOW_ROWS = 80


def _exclusive_count(mask, tri):
    E, S = mask.shape
    mb = mask.astype(BF16)
    carry = jnp.zeros((E, 1), F32)
    parts = []
    for c in range(0, S, SCAN):
        chunk = mb[:, c:c + SCAN]
        parts.append(jnp.dot(chunk, tri, preferred_element_type=F32) + carry)
        carry = carry + jnp.sum(chunk.astype(F32), axis=1, keepdims=True)
    return jnp.concatenate(parts, axis=1)


def _route_body(cap, lg_ref, tri_ref, slot_ref, aff_ref, start_ref):
    logits = lg_ref[...]
    mx = jnp.max(logits, axis=0, keepdims=True)
    ex = jnp.exp(logits - mx)
    aff = ex / jnp.sum(ex, axis=0, keepdims=True)
    aff_ref[...] = aff
    bits = pltpu.bitcast(aff, jnp.int32)
    E = bits.shape[0]

    def count_ge(cand):
        return jnp.sum((bits >= cand).astype(F32), axis=1, keepdims=True)

    def step(k, prefix):
        shift = VALUE_BITS - 1 - RADIX_BITS * (k + 1)
        out = prefix
        for digit in range(1, 2 ** RADIX_BITS):
            cand = prefix | lax.shift_left(jnp.int32(digit), shift)
            out = jnp.where(count_ge(cand) >= cap, cand, out)
        return out

    top = jnp.full((E, 1), 1 << (VALUE_BITS - 2), jnp.int32)
    prefix = jnp.where(count_ge(top) >= cap, top, 0)
    thresh = lax.fori_loop(0, (VALUE_BITS - 2) // RADIX_BITS, step, prefix)
    gt = bits > thresh
    eq = bits == thresh
    need = cap - jnp.sum(gt.astype(F32), axis=1, keepdims=True)
    tri = tri_ref[...]
    sel = gt | (eq & (_exclusive_count(eq, tri) < need))
    pos = _exclusive_count(sel, tri)
    slot_ref[...] = jnp.where(sel, pos, -1.0).astype(jnp.int32)
    S = pos.shape[1]
    start_ref[0] = jnp.concatenate([pos[:, c:c + 1] for c in range(0, S, CHUNK)], axis=1).astype(jnp.int32)


def _route(logits_t, B, S, cap):
    E = logits_t.shape[0]
    tri = jnp.asarray(np.triu(np.ones((SCAN, SCAN), np.float32), 1), BF16)
    spec = pl.BlockSpec((E, S), lambda b: (0, b))
    return pl.pallas_call(
        functools.partial(_route_body, cap),
        grid=(B,),
        in_specs=[spec, _const_spec(tri.shape)],
        out_specs=[spec, spec, pl.BlockSpec((1, E, S // CHUNK), lambda b: (b, 0, 0))],
        out_shape=[jax.ShapeDtypeStruct((E, B * S), jnp.int32), jax.ShapeDtypeStruct((E, B * S), F32),
                   jax.ShapeDtypeStruct((B, E, S // CHUNK), jnp.int32)],
        compiler_params=_params(("parallel",)),
        name="route",
    )(logits_t, tri)


def _window(starts_ref, first, kt, n_chunks, cap):
    start = starts_ref[first + kt]
    nxt = starts_ref[first + jnp.minimum(kt + 1, n_chunks - 1)]
    end = jnp.where(kt + 1 < n_chunks, nxt, cap)
    base = jnp.minimum(lax.shift_left(lax.shift_right_logical(start, 4), 4), cap - WINDOW_ROWS)
    return pl.multiple_of(base, 16), end <= base + WINDOW_ROWS


def _dispatch_body(cap, starts_ref, slot_ref, aff_ref, x_ref, xe_ref, tv_ref):
    b = pl.program_id(0)
    E, S = slot_ref.shape
    n_chunks = S // CHUNK
    windows = [[_window(starts_ref, (b * E + e) * n_chunks, kt, n_chunks, cap) for e in range(E)]
               for kt in range(n_chunks)]
    fits = functools.reduce(jnp.logical_and, [w[1] for per_chunk in windows for w in per_chunk])

    @pl.when(fits)
    def _():
        xe_ref[...] = jnp.zeros_like(xe_ref)
        tv_ref[...] = jnp.zeros_like(tv_ref)
        ri = lax.broadcasted_iota(jnp.int32, (WINDOW_ROWS, CHUNK), 0)
        for kt in range(n_chunks):
            cols = slice(kt * CHUNK, (kt + 1) * CHUNK)
            hits = [(ri + windows[kt][e][0]) == slot_ref[e:e + 1, cols] for e in range(E)]
            onehot = jnp.concatenate([jnp.where(h, 1.0, 0.0).astype(BF16) for h in hits], axis=0)
            rows_all = jnp.dot(onehot, x_ref[cols, :], preferred_element_type=F32)
            for e in range(E):
                rows = pl.ds(windows[kt][e][0], WINDOW_ROWS)
                xe_ref[0, e, rows, :] += rows_all[e * WINDOW_ROWS:(e + 1) * WINDOW_ROWS].astype(BF16)
                tv = jnp.sum(jnp.where(hits[e], aff_ref[e:e + 1, cols], 0.0), axis=1, keepdims=True)
                tv_ref[0, e, rows, :] += jnp.broadcast_to(tv, (WINDOW_ROWS, LANES))

    @pl.when(jnp.logical_not(fits))
    def _():
        ri = lax.broadcasted_iota(jnp.int32, (cap, S), 0)
        for e in range(E):
            hit = ri == slot_ref[e:e + 1, :]
            onehot = jnp.where(hit, 1.0, 0.0).astype(BF16)
            xe_ref[0, e] = jnp.dot(onehot, x_ref[...], preferred_element_type=F32).astype(BF16)
            tv = jnp.sum(jnp.where(hit, aff_ref[e:e + 1, :], 0.0), axis=1, keepdims=True)
            tv_ref[0, e] = jnp.broadcast_to(tv, (cap, LANES))


def _dispatch(starts, slot, aff, x1b, B, S, cap):
    E = slot.shape[0]
    D = x1b.shape[1]
    row = pl.BlockSpec((E, S), lambda b, st: (0, b))
    return pl.pallas_call(
        functools.partial(_dispatch_body, cap),
        grid_spec=pltpu.PrefetchScalarGridSpec(
            num_scalar_prefetch=1,
            grid=(B,),
            in_specs=[row, row, pl.BlockSpec((S, D), lambda b, st: (b, 0))],
            out_specs=[pl.BlockSpec((1, E, cap, D), lambda b, st: (b, 0, 0, 0)),
                       pl.BlockSpec((1, E, cap, LANES), lambda b, st: (b, 0, 0, 0))]),
        out_shape=[jax.ShapeDtypeStruct((B, E, cap, D), BF16),
                   jax.ShapeDtypeStruct((B, E, cap, LANES), F32)],
        compiler_params=_params(("arbitrary",)),
        name="dispatch",
    )(starts, slot, aff, x1b)


EXPERT_SEQS = 4


def _experts_body(xe_ref, tv_ref, wg_ref, wu_ref, wd_ref, y_ref):
    nb, _, cap, D = xe_ref.shape
    xe = xe_ref[...].reshape(nb * cap, D)
    tv = tv_ref[...].reshape(nb * cap, LANES)[:, 0:1]
    gate = jnp.dot(xe, wg_ref[0].astype(BF16), preferred_element_type=F32)
    up = jnp.dot(xe, wu_ref[0].astype(BF16), preferred_element_type=F32)
    h = (gate * jax.nn.sigmoid(gate) * up).astype(BF16)
    ye = jnp.dot(h, wd_ref[0].astype(BF16), preferred_element_type=F32) * tv
    y_ref[...] = ye.astype(BF16).reshape(nb, 1, cap, D)


def _experts(xe, tv, wg, wu, wd):
    B, E, cap, D = xe.shape
    F = wg.shape[2]
    nb = EXPERT_SEQS
    wspec = lambda shape: pl.BlockSpec(shape, lambda e, b: (e, 0, 0), pipeline_mode=pl.Buffered(1))
    tok = lambda width: pl.BlockSpec((nb, 1, cap, width), lambda e, b: (b, e, 0, 0))
    return pl.pallas_call(
        _experts_body,
        grid=(E, B // nb),
        in_specs=[tok(D), tok(LANES), wspec((1, D, F)), wspec((1, D, F)),
                  pl.BlockSpec((1, F, D), lambda e, b: (e, 0, 0))],
        out_specs=tok(D),
        out_shape=jax.ShapeDtypeStruct((B, E, cap, D), BF16),
        compiler_params=_params(("arbitrary", "arbitrary")),
        name="experts",
    )(xe, tv, wg, wu, wd)


def _combine_body(cap, tiles_per_seq, starts_ref, slot_ref, y_ref, x1_ref, g_ref, b_ref, o_ref):
    i = pl.program_id(0)
    E = slot_ref.shape[0]
    tm = slot_ref.shape[1]
    subs = tm // CHUNK
    n_chunks = tiles_per_seq * subs
    b = i // tiles_per_seq
    nt = (((0,), (0,)), ((), ()))
    windows = [[_window(starts_ref, (b * E + e) * n_chunks, (i % tiles_per_seq) * subs + sub, n_chunks, cap)
                for e in range(E)] for sub in range(subs)]
    fits = functools.reduce(jnp.logical_and, [w[1] for per_sub in windows for w in per_sub])

    def finish(sub, f):
        rows = pl.ds(sub * CHUNK, CHUNK)
        o_ref[rows, :] = _layer_norm(DEEPNORM_ALPHA * x1_ref[rows, :] + f, g_ref[...], b_ref[...])

    @pl.when(fits)
    def _():
        ri = lax.broadcasted_iota(jnp.int32, (WINDOW_ROWS, CHUNK), 0)
        fs = []
        for sub in range(subs):
            cols = slice(sub * CHUNK, (sub + 1) * CHUNK)
            onehot = jnp.concatenate(
                [jnp.where(ri + windows[sub][e][0] == slot_ref[e:e + 1, cols], 1.0, 0.0).astype(BF16)
                 for e in range(E)], axis=0)
            ywin = jnp.concatenate(
                [y_ref[0, e, pl.ds(windows[sub][e][0], WINDOW_ROWS), :] for e in range(E)], axis=0)
            fs.append(lax.dot_general(onehot, ywin, nt, preferred_element_type=F32))
        for sub in range(subs):
            finish(sub, fs[sub])

    @pl.when(jnp.logical_not(fits))
    def _():
        ri = lax.broadcasted_iota(jnp.int32, (cap, CHUNK), 0)
        yall = jnp.concatenate([y_ref[0, e] for e in range(E)], axis=0)
        for sub in range(subs):
            cols = slice(sub * CHUNK, (sub + 1) * CHUNK)
            onehot = jnp.concatenate(
                [jnp.where(ri == slot_ref[e:e + 1, cols], 1.0, 0.0).astype(BF16) for e in range(E)], axis=0)
            finish(sub, lax.dot_general(onehot, yall, nt, preferred_element_type=F32))


def _combine(starts, slot, y, x1, g, b, B, S, cap):
    T, D = x1.shape
    E = slot.shape[0]
    tm = WIDE_TILE
    tiles_per_seq = S // tm
    const = lambda shape: pl.BlockSpec(shape, lambda i, st: (0,) * len(shape), pipeline_mode=pl.Buffered(1))
    return pl.pallas_call(
        functools.partial(_combine_body, cap, tiles_per_seq),
        grid_spec=pltpu.PrefetchScalarGridSpec(
            num_scalar_prefetch=1,
            grid=(T // tm,),
            in_specs=[pl.BlockSpec((E, tm), lambda i, st: (0, i)),
                      pl.BlockSpec((1, E, cap, D), lambda i, st: (i // tiles_per_seq, 0, 0, 0)),
                      pl.BlockSpec((tm, D), lambda i, st: (i, 0)),
                      const(g.shape), const(b.shape)],
            out_specs=pl.BlockSpec((tm, D), lambda i, st: (i, 0))),
        out_shape=jax.ShapeDtypeStruct((T, D), F32),
        compiler_params=_params(("arbitrary",)),
        name="combine",
    )(starts, slot, y, x1, g, b)


def _split_hi_lo(w):
    hi = _truncate_to_bf16(w)
    return jnp.concatenate([hi.astype(BF16), (w - hi).astype(BF16)], axis=0)


def kernel(x, ln0_g, ln0_b, w_in, conv_w, attn_sink, w_attn_o, w_conv_o, w_out, ln1_g, ln1_b,
           w_router, w_gate, w_up, w_down, ln2_g, ln2_b):
    B, S, D = x.shape
    assert w_in.shape[0] == DEPTH == 1
    assert S % TOKEN_TILE == 0 and TOKEN_TILE % SUB == 0 and TOKEN_TILE % CHUNK == 0
    assert B % EXPERT_SEQS == 0 and S % WIDE_TILE == 0 and WIDE_TILE % CHUNK == 0
    T = B * S
    d_attn = N_HEADS * HEAD_DIM
    d_kv = N_KV_HEADS * HEAD_DIM
    cap = CAPACITY_FACTOR * S // N_EXPERTS
    row = lambda a: a.reshape(1, -1).astype(F32)

    splits = ((d_attn, HEAD_DIM ** -0.5 * LOG2E), (d_kv, 1.0), (d_kv, 1.0),
              (D, 1.0), (D, 1.0), (D, 1.0), (D, 1.0), (D, 1.0))
    xn, q, k, v, cb, cc, cx, ga, gc = _inproj(
        x.reshape(T, D), row(ln0_g), row(ln0_b), w_in[0].astype(BF16), splits)

    o = _attention(q, k, v, attn_sink[0].astype(F32) * LOG2E, B, S)

    x1, x1b, logits_t = _mixer_out(
        o, cb, cc, cx, ga, gc, xn, conv_w[0].astype(F32),
        w_attn_o[0].astype(BF16), w_conv_o[0].astype(BF16), w_out[0].astype(BF16),
        row(ln1_g[0]), row(ln1_b[0]), _split_hi_lo(w_router[0].T.astype(F32)), S)

    slot, aff, starts = _route(logits_t, B, S, cap)
    starts = starts.reshape(-1)

    xe, top_val = _dispatch(starts, slot, aff, x1b, B, S, cap)

    y = _experts(xe, top_val, w_gate[0], w_up[0], w_down[0])

    out = _combine(starts, slot, y, x1, row(ln2_g[0]), row(ln2_b[0]), B, S, cap)
    return out.reshape(B, S, D)
```

```python
import functools

import jax
import jax.numpy as jnp
import numpy as np
from jax import lax
from jax.experimental import pallas as pl
from jax.experimental.pallas import tpu as pltpu

F32 = jnp.float32
BF16 = jnp.bfloat16

N_HEADS = 16
N_KV_HEADS = 4
HEAD_DIM = 64
REP = N_HEADS // N_KV_HEADS
WINDOW = 128
BLOCK = 128
NEG_INF = -1e30
LOG2E = float(np.log2(np.e))
CONV_WIDTH = 3
N_EXPERTS = 16
CAPACITY_FACTOR = 2
DEPTH = 1
DEEPNORM_ALPHA = (2.0 * DEPTH) ** 0.25
LN_EPS = 1e-5

V7X_VMEM_BYTES = 64 * 1024 * 1024
LANES = 128
VMEM_LIMIT = 56 * 1024 * 1024
EXPERTS_VMEM_LIMIT = V7X_VMEM_BYTES - 2 * 1024 * 1024

TOKEN_TILE = 512
WIDE_TILE = 1024
SUB = BLOCK


def _layer_norm(x, g, b):
    mu = jnp.mean(x, axis=-1, keepdims=True)
    xc = x - mu
    var = jnp.mean(xc * xc, axis=-1, keepdims=True)
    return xc * lax.rsqrt(var + LN_EPS) * g + b


def _truncate_to_bf16(x):
    bits = lax.bitcast_convert_type(x, jnp.uint32) & jnp.uint32(0xFFFF0000)
    return lax.bitcast_convert_type(bits, F32)


def _params(sem):
    return pltpu.CompilerParams(dimension_semantics=sem, vmem_limit_bytes=VMEM_LIMIT)


def _const_spec(shape):
    zeros = (0,) * len(shape)
    return pl.BlockSpec(shape, lambda *_: zeros, pipeline_mode=pl.Buffered(1))


INPROJ_PARTS = 4


def _inproj_body(splits, x_ref, g_ref, b_ref, w_ref, xn_ref, *out_refs):
    pm = x_ref.shape[0] // INPROJ_PARTS
    groups = []
    for p in range(INPROJ_PARTS):
        r = pl.ds(p * pm, pm)
        xn = _layer_norm(x_ref[r, :], g_ref[...], b_ref[...])
        xn_ref[r, :] = xn
        groups.append((r, xn.astype(BF16)))
    for r, xb in groups:
        off = 0
        for (width, scale), o_ref in zip(splits, out_refs):
            chunk = min(width, 512)
            for c in range(0, width, chunk):
                acc = jnp.dot(xb, w_ref[:, off + c:off + c + chunk], preferred_element_type=F32)
                if scale != 1.0:
                    acc = acc * scale
                o_ref[r, c:c + chunk] = acc.astype(BF16)
            off += width


def _inproj(x2, g, b, w_bf16, splits):
    T, D = x2.shape
    n_in = w_bf16.shape[1]
    tm = TOKEN_TILE
    out_shape = [jax.ShapeDtypeStruct((T, D), F32)]
    out_specs = [pl.BlockSpec((tm, D), lambda i: (i, 0))]
    for width, _ in splits:
        out_shape.append(jax.ShapeDtypeStruct((T, width), BF16))
        out_specs.append(pl.BlockSpec((tm, width), lambda i: (i, 0)))
    return pl.pallas_call(
        functools.partial(_inproj_body, splits),
        grid=(T // tm,),
        in_specs=[
            pl.BlockSpec((tm, D), lambda i: (i, 0)),
            _const_spec((1, D)),
            _const_spec((1, D)),
            _const_spec((D, n_in)),
        ],
        out_specs=out_specs,
        out_shape=out_shape,
        compiler_params=_params(("parallel",)),
        name="inproj",
    )(x2, g, b, w_bf16)


def _alibi_slopes():
    return (2.0 ** (-8.0 * np.arange(1, N_HEADS + 1) / N_HEADS)).astype(np.float32)


def _attention_bias():
    slopes = _alibi_slopes().reshape(N_KV_HEADS, REP)
    i = np.arange(SUB)[None, :]
    j = np.arange(SUB)[:, None]
    out = np.empty((5, N_KV_HEADS, SUB, REP * SUB), np.float32)
    for var, (shift, masked) in enumerate([(-SUB, False), (-SUB, True), (0, False), (SUB, False), (SUB, True)]):
        dist = np.abs(i - (j + shift))
        valid = (dist <= WINDOW) & (not masked)
        for g in range(N_KV_HEADS):
            for r in range(REP):
                b = (-slopes[g, r] * dist.astype(np.float32)).astype(np.float64) * LOG2E
                out[var, g, :, r * SUB:(r + 1) * SUB] = np.where(valid, b, NEG_INF).astype(np.float32)
    return out


ONES_ROWS = 16


def _attention_body(n_sub, sink_ref, q_ref, kp_ref, kc_ref, kn_ref, vp_ref, vc_ref, vn_ref,
                    bias_ref, o_ref):
    n = pl.program_id(1)
    first = n == 0
    last = n == pl.num_programs(1) - 1
    kwin = jnp.concatenate([kp_ref[...], kc_ref[...], kn_ref[...]], axis=0)
    vwin = jnp.concatenate([vp_ref[...], vc_ref[...], vn_ref[...]], axis=0)
    vwin_t = vwin.astype(F32).T.astype(BF16)
    ones = jnp.ones((ONES_ROWS, vwin_t.shape[1]), BF16)

    def scores(j, g):
        qj = q_ref[j * SUB:(j + 1) * SUB, :]
        prev_var = jnp.where(first, 1, 0) if j == 0 else 0
        next_var = jnp.where(last, 4, 3) if j == n_sub - 1 else 3
        qg = jnp.concatenate(
            [qj[:, (g * REP + r) * HEAD_DIM:(g * REP + r + 1) * HEAD_DIM] for r in range(REP)], axis=0)
        kg = kwin[j * SUB:(j + 3) * SUB, g * HEAD_DIM:(g + 1) * HEAD_DIM]
        s = lax.dot_general(kg, qg, (((1,), (1,)), ((), ())), preferred_element_type=F32)
        bias = jnp.concatenate([bias_ref[prev_var, g], bias_ref[2, g], bias_ref[next_var, g]], axis=0)
        s = s + bias
        sink = jnp.concatenate(
            [jnp.full((1, SUB), sink_ref[g * REP + r], F32) for r in range(REP)], axis=1)
        m = jnp.maximum(jnp.max(s, axis=0, keepdims=True), sink)
        return s, m, sink

    def probs(s, m):
        return jnp.exp2(s - m).astype(BF16)

    def weighted_values(j, g, p, m, sink):
        vg_t = jnp.concatenate([vwin_t[g * HEAD_DIM:(g + 1) * HEAD_DIM], ones], axis=0)
        vg_t = vg_t[:, j * SUB:(j + 3) * SUB]
        ov = jnp.dot(vg_t, p, preferred_element_type=F32)
        denom = ov[HEAD_DIM:HEAD_DIM + 1] + jnp.exp2(sink - m)
        return ov[:HEAD_DIM] / denom

    units = [(j, g) for j in range(n_sub) for g in range(N_KV_HEADS)]
    stage1, stage2 = {}, {}
    heads_t = {j: [] for j in range(n_sub)}
    for step in range(len(units) + 2):
        if step >= 2:
            j, g = units[step - 2]
            p, m, sink = stage2.pop(step - 2)
            og_t = weighted_values(j, g, p, m, sink)
            heads_t[j] += [og_t[:, r * SUB:(r + 1) * SUB] for r in range(REP)]
            if g == N_KV_HEADS - 1:
                o_ref[j * SUB:(j + 1) * SUB, :] = jnp.concatenate(heads_t.pop(j), axis=0).T.astype(BF16)
        if 1 <= step <= len(units):
            s, m, sink = stage1.pop(step - 1)
            stage2[step - 1] = (probs(s, m), m, sink)
        if step < len(units):
            stage1[step] = scores(*units[step])


def _attention(q, k, v, sink, B, S):
    T, d_attn = q.shape
    d_kv = k.shape[1]
    tq = WIDE_TILE
    n_sub = tq // SUB
    nq = S // tq
    nblk = S // SUB
    bias = jnp.asarray(_attention_bias())

    def q_map(b, n):
        return (b * nq + n, 0)

    def prev_map(b, n):
        return (b * nblk + jnp.maximum(n * n_sub - 1, 0), 0)

    def next_map(b, n):
        return (b * nblk + jnp.minimum((n + 1) * n_sub, nblk - 1), 0)

    kv_specs = [pl.BlockSpec((SUB, d_kv), prev_map), pl.BlockSpec((tq, d_kv), q_map),
                pl.BlockSpec((SUB, d_kv), next_map)]
    return pl.pallas_call(
        functools.partial(_attention_body, n_sub),
        grid=(B, nq),
        in_specs=[pl.BlockSpec(memory_space=pltpu.SMEM),
                  pl.BlockSpec((tq, d_attn), q_map)] + kv_specs + kv_specs + [_const_spec(bias.shape)],
        out_specs=pl.BlockSpec((tq, d_attn), q_map),
        out_shape=jax.ShapeDtypeStruct((T, d_attn), BF16),
        compiler_params=_params(("parallel", "parallel")),
        name="attention",
    )(sink, q, k, k, k, v, v, v, bias)


HALO = 16
MIXER_PARTS = 2


def _mixer_out_body(tiles_per_seq, o_ref, cb_ref, cc_ref, cx_ref, ccp_ref, cxp_ref, ccn_ref, cxn_ref,
                    ga_ref, gc_ref, xn_ref, cw_ref, wa_ref, wc_ref, wo_ref, g_ref, b_ref, wr_ref,
                    x1_ref, x1b_ref, lg_ref):
    i = pl.program_id(0)
    n = i % tiles_per_seq
    tm = cc_ref.shape[0]
    pm = tm // MIXER_PARTS
    u = cc_ref[...].astype(F32) * cx_ref[...].astype(F32)
    u_before = ccp_ref[HALO - 1:HALO, :].astype(F32) * cxp_ref[HALO - 1:HALO, :].astype(F32)
    u_after = ccn_ref[0:1, :].astype(F32) * cxn_ref[0:1, :].astype(F32)
    u_before = jnp.where(n == 0, 0.0, u_before)
    u_after = jnp.where(n == tiles_per_seq - 1, 0.0, u_after)
    cw = cw_ref[...]
    row = lax.broadcasted_iota(jnp.int32, (pm, u.shape[1]), 0)

    def project_in(p):
        r = pl.ds(p * pm, pm)
        y_attn = jnp.dot(o_ref[r, :], wa_ref[...], preferred_element_type=F32)
        up = u[p * pm:(p + 1) * pm]
        before = u_before if p == 0 else u[p * pm - 1:p * pm]
        after = u_after if p == MIXER_PARTS - 1 else u[(p + 1) * pm:(p + 1) * pm + 1]
        um1 = jnp.where(row == 0, before, pltpu.roll(up, 1, 0))
        up1 = jnp.where(row == pm - 1, after, pltpu.roll(up, pm - 1, 0))
        y = cb_ref[r, :].astype(F32) * (cw[0:1, :] * um1 + cw[1:2, :] * up + cw[2:3, :] * up1)
        y_conv = jnp.dot(y.astype(BF16), wc_ref[...], preferred_element_type=F32)
        return y_attn, y_conv

    def project_out(p, y_attn, y_conv):
        r = pl.ds(p * pm, pm)
        merged = (jax.nn.sigmoid(ga_ref[r, :].astype(F32)) * y_attn
                  + jax.nn.sigmoid(gc_ref[r, :].astype(F32)) * y_conv)
        return jnp.dot(merged.astype(BF16), wo_ref[...], preferred_element_type=F32)

    def finish(p, h):
        r = pl.ds(p * pm, pm)
        x1 = _layer_norm(DEEPNORM_ALPHA * xn_ref[r, :] + h, g_ref[...], b_ref[...])
        x1_ref[r, :] = x1
        x1b_ref[r, :] = x1.astype(BF16)
        x1_top = _truncate_to_bf16(x1)
        x1_hi = x1_top.astype(BF16)
        x1_lo = (x1 - x1_top).astype(BF16)
        nt = (((1,), (1,)), ((), ()))
        E = lg_ref.shape[0]
        by_hi = lax.dot_general(wr_ref[...], x1_hi, nt, preferred_element_type=F32)
        by_lo = lax.dot_general(wr_ref[0:E, :], x1_lo, nt, preferred_element_type=F32)
        lg_ref[:, r] = by_hi[0:E] + (by_hi[E:2 * E] + by_lo)

    stage1, stage2 = {}, {}
    for step in range(MIXER_PARTS + 2):
        if step < MIXER_PARTS:
            stage1[step] = project_in(step)
        if 0 <= step - 1 < MIXER_PARTS:
            stage2[step - 1] = project_out(step - 1, *stage1.pop(step - 1))
        if 0 <= step - 2 < MIXER_PARTS:
            finish(step - 2, stage2.pop(step - 2))


def _mixer_out(o, cb, cc, cx, ga, gc, xn, conv_w, wa, wc, wo, g, b, wr_t, S):
    T, D = xn.shape
    tm = TOKEN_TILE
    tiles_per_seq = S // tm
    hb = tm // HALO
    n_halo = T // HALO
    E = wr_t.shape[0] // 2

    def tile(i):
        return (i, 0)

    def before(i):
        return (jnp.maximum(i * hb - 1, 0), 0)

    def after(i):
        return (jnp.minimum((i + 1) * hb, n_halo - 1), 0)

    tspec = pl.BlockSpec((tm, D), tile)
    return pl.pallas_call(
        functools.partial(_mixer_out_body, tiles_per_seq),
        grid=(T // tm,),
        in_specs=[tspec, tspec, tspec, tspec,
                  pl.BlockSpec((HALO, D), before), pl.BlockSpec((HALO, D), before),
                  pl.BlockSpec((HALO, D), after), pl.BlockSpec((HALO, D), after),
                  tspec, tspec, tspec,
                  _const_spec(conv_w.shape), _const_spec(wa.shape), _const_spec(wc.shape),
                  _const_spec(wo.shape), _const_spec(g.shape), _const_spec(b.shape),
                  _const_spec(wr_t.shape)],
        out_specs=[tspec, tspec, pl.BlockSpec((E, tm), lambda i: (0, i))],
        out_shape=[jax.ShapeDtypeStruct((T, D), F32), jax.ShapeDtypeStruct((T, D), BF16),
                   jax.ShapeDtypeStruct((E, T), F32)],
        compiler_params=_params(("parallel",)),
        name="mixer_out",
    )(o, cb, cc, cx, cc, cx, cc, cx, ga, gc, xn, conv_w, wa, wc, wo, g, b, wr_t)


VALUE_BITS = 32
RADIX_BITS = 3
SCAN = 256
CHUNK = 256
WINDOW_ROWS = 80


def _exclusive_count(mask, tri):
    E, S = mask.shape
    mb = mask.astype(BF16)
    carry = jnp.zeros((E, 1), F32)
    parts = []
    for c in range(0, S, SCAN):
        chunk = mb[:, c:c + SCAN]
        parts.append(jnp.dot(chunk, tri, preferred_element_type=F32) + carry)
        carry = carry + jnp.sum(chunk.astype(F32), axis=1, keepdims=True)
    return jnp.concatenate(parts, axis=1)


def _route_body(cap, lg_ref, tri_ref, slot_ref, aff_ref, start_ref):
    logits = lg_ref[...]
    mx = jnp.max(logits, axis=0, keepdims=True)
    ex = jnp.exp(logits - mx)
    aff = ex / jnp.sum(ex, axis=0, keepdims=True)
    aff_ref[...] = aff
    bits = pltpu.bitcast(aff, jnp.int32)
    E = bits.shape[0]

    def count_ge(cand):
        return jnp.sum((bits >= cand).astype(F32), axis=1, keepdims=True)

    def step(k, prefix):
        shift = VALUE_BITS - 1 - RADIX_BITS * (k + 1)
        out = prefix
        for digit in range(1, 2 ** RADIX_BITS):
            cand = prefix | lax.shift_left(jnp.int32(digit), shift)
            out = jnp.where(count_ge(cand) >= cap, cand, out)
        return out

    top = jnp.full((E, 1), 1 << (VALUE_BITS - 2), jnp.int32)
    prefix = jnp.where(count_ge(top) >= cap, top, 0)
    thresh = lax.fori_loop(0, (VALUE_BITS - 2) // RADIX_BITS, step, prefix)
    gt = bits > thresh
    eq = bits == thresh
    need = cap - jnp.sum(gt.astype(F32), axis=1, keepdims=True)
    tri = tri_ref[...]
    sel = gt | (eq & (_exclusive_count(eq, tri) < need))
    pos = _exclusive_count(sel, tri)
    slot_ref[...] = jnp.where(sel, pos, -1.0).astype(jnp.int32)
    S = pos.shape[1]
    start_ref[0] = jnp.concatenate([pos[:, c:c + 1] for c in range(0, S, CHUNK)], axis=1).astype(jnp.int32)


def _route(logits_t, B, S, cap):
    E = logits_t.shape[0]
    tri = jnp.asarray(np.triu(np.ones((SCAN, SCAN), np.float32), 1), BF16)
    spec = pl.BlockSpec((E, S), lambda b: (0, b))
    return pl.pallas_call(
        functools.partial(_route_body, cap),
        grid=(B,),
        in_specs=[spec, _const_spec(tri.shape)],
        out_specs=[spec, spec, pl.BlockSpec((1, E, S // CHUNK), lambda b: (b, 0, 0))],
        out_shape=[jax.ShapeDtypeStruct((E, B * S), jnp.int32), jax.ShapeDtypeStruct((E, B * S), F32),
                   jax.ShapeDtypeStruct((B, E, S // CHUNK), jnp.int32)],
        compiler_params=_params(("parallel",)),
        name="route",
    )(logits_t, tri)


def _window(starts_ref, first, kt, n_chunks, cap):
    start = starts_ref[first + kt]
    nxt = starts_ref[first + jnp.minimum(kt + 1, n_chunks - 1)]
    end = jnp.where(kt + 1 < n_chunks, nxt, cap)
    base = jnp.minimum(lax.shift_left(lax.shift_right_logical(start, 4), 4), cap - WINDOW_ROWS)
    return pl.multiple_of(base, 16), end <= base + WINDOW_ROWS


def _dispatch_body(cap, starts_ref, slot_ref, aff_ref, x_ref, xe_ref, tv_ref):
    b = pl.program_id(0)
    E, S = slot_ref.shape
    n_chunks = S // CHUNK
    windows = [[_window(starts_ref, (b * E + e) * n_chunks, kt, n_chunks, cap) for e in range(E)]
               for kt in range(n_chunks)]
    fits = functools.reduce(jnp.logical_and, [w[1] for per_chunk in windows for w in per_chunk])

    @pl.when(fits)
    def _():
        xe_ref[...] = jnp.zeros_like(xe_ref)
        tv_ref[...] = jnp.zeros_like(tv_ref)
        ri = lax.broadcasted_iota(jnp.int32, (WINDOW_ROWS, CHUNK), 0)
        for kt in range(n_chunks):
            cols = slice(kt * CHUNK, (kt + 1) * CHUNK)
            hits = [(ri + windows[kt][e][0]) == slot_ref[e:e + 1, cols] for e in range(E)]
            onehot = jnp.concatenate([jnp.where(h, 1.0, 0.0).astype(BF16) for h in hits], axis=0)
            rows_all = jnp.dot(onehot, x_ref[cols, :], preferred_element_type=F32)
            for e in range(E):
                rows = pl.ds(windows[kt][e][0], WINDOW_ROWS)
                xe_ref[0, e, rows, :] += rows_all[e * WINDOW_ROWS:(e + 1) * WINDOW_ROWS].astype(BF16)
                tv = jnp.sum(jnp.where(hits[e], aff_ref[e:e + 1, cols], 0.0), axis=1, keepdims=True)
                tv_ref[0, e, rows, :] += jnp.broadcast_to(tv, (WINDOW_ROWS, LANES))

    @pl.when(jnp.logical_not(fits))
    def _():
        ri = lax.broadcasted_iota(jnp.int32, (cap, S), 0)
        for e in range(E):
            hit = ri == slot_ref[e:e + 1, :]
            onehot = jnp.where(hit, 1.0, 0.0).astype(BF16)
            xe_ref[0, e] = jnp.dot(onehot, x_ref[...], preferred_element_type=F32).astype(BF16)
            tv = jnp.sum(jnp.where(hit, aff_ref[e:e + 1, :], 0.0), axis=1, keepdims=True)
            tv_ref[0, e] = jnp.broadcast_to(tv, (cap, LANES))


def _dispatch(starts, slot, aff, x1b, B, S, cap):
    E = slot.shape[0]
    D = x1b.shape[1]
    row = pl.BlockSpec((E, S), lambda b, st: (0, b))
    return pl.pallas_call(
        functools.partial(_dispatch_body, cap),
        grid_spec=pltpu.PrefetchScalarGridSpec(
            num_scalar_prefetch=1,
            grid=(B,),
            in_specs=[row, row, pl.BlockSpec((S, D), lambda b, st: (b, 0))],
            out_specs=[pl.BlockSpec((1, E, cap, D), lambda b, st: (b, 0, 0, 0)),
                       pl.BlockSpec((1, E, cap, LANES), lambda b, st: (b, 0, 0, 0))]),
        out_shape=[jax.ShapeDtypeStruct((B, E, cap, D), BF16),
                   jax.ShapeDtypeStruct((B, E, cap, LANES), F32)],
        compiler_params=_params(("arbitrary",)),
        name="dispatch",
    )(starts, slot, aff, x1b)


EXPERT_SEQS = 4


def _experts_body(xe_ref, tv_ref, wg_ref, wu_ref, wd_ref, y_ref):
    nb, _, cap, D = xe_ref.shape
    xe = xe_ref[...].reshape(nb * cap, D)
    tv = tv_ref[...].reshape(nb * cap, LANES)[:, 0:1]
    gate = jnp.dot(xe, wg_ref[0].astype(BF16), preferred_element_type=F32)
    up = jnp.dot(xe, wu_ref[0].astype(BF16), preferred_element_type=F32)
    h = (gate * jax.nn.sigmoid(gate) * up).astype(BF16)
    ye = jnp.dot(h, wd_ref[0].astype(BF16), preferred_element_type=F32) * tv
    y_ref[...] = ye.astype(BF16).reshape(nb, 1, cap, D)


def _experts(xe, tv, wg, wu, wd):
    B, E, cap, D = xe.shape
    F = wg.shape[2]
    nb = EXPERT_SEQS
    wspec = lambda shape: pl.BlockSpec(shape, lambda e, b: (e, 0, 0), pipeline_mode=pl.Buffered(1))
    tok = lambda width: pl.BlockSpec((nb, 1, cap, width), lambda e, b: (b, e, 0, 0))
    return pl.pallas_call(
        _experts_body,
        grid=(E, B // nb),
        in_specs=[tok(D), tok(LANES), wspec((1, D, F)), pl.BlockSpec((1, D, F), lambda e, b: (e, 0, 0)),
                  pl.BlockSpec((1, F, D), lambda e, b: (e, 0, 0))],
        out_specs=tok(D),
        out_shape=jax.ShapeDtypeStruct((B, E, cap, D), BF16),
        compiler_params=pltpu.CompilerParams(dimension_semantics=("arbitrary", "arbitrary"),
                                             vmem_limit_bytes=EXPERTS_VMEM_LIMIT),
        name="experts",
    )(xe, tv, wg, wu, wd)


def _combine_body(cap, tiles_per_seq, starts_ref, slot_ref, y_ref, x1_ref, g_ref, b_ref, o_ref):
    i = pl.program_id(0)
    E = slot_ref.shape[0]
    tm = slot_ref.shape[1]
    subs = tm // CHUNK
    n_chunks = tiles_per_seq * subs
    b = i // tiles_per_seq
    nt = (((0,), (0,)), ((), ()))
    windows = [[_window(starts_ref, (b * E + e) * n_chunks, (i % tiles_per_seq) * subs + sub, n_chunks, cap)
                for e in range(E)] for sub in range(subs)]
    fits = functools.reduce(jnp.logical_and, [w[1] for per_sub in windows for w in per_sub])

    def finish(sub, f):
        rows = pl.ds(sub * CHUNK, CHUNK)
        o_ref[rows, :] = _layer_norm(DEEPNORM_ALPHA * x1_ref[rows, :] + f, g_ref[...], b_ref[...])

    @pl.when(fits)
    def _():
        ri = lax.broadcasted_iota(jnp.int32, (WINDOW_ROWS, CHUNK), 0)
        fs = []
        for sub in range(subs):
            cols = slice(sub * CHUNK, (sub + 1) * CHUNK)
            onehot = jnp.concatenate(
                [jnp.where(ri + windows[sub][e][0] == slot_ref[e:e + 1, cols], 1.0, 0.0).astype(BF16)
                 for e in range(E)], axis=0)
            ywin = jnp.concatenate(
                [y_ref[0, e, pl.ds(windows[sub][e][0], WINDOW_ROWS), :] for e in range(E)], axis=0)
            fs.append(lax.dot_general(onehot, ywin, nt, preferred_element_type=F32))
        for sub in range(subs):
            finish(sub, fs[sub])

    @pl.when(jnp.logical_not(fits))
    def _():
        ri = lax.broadcasted_iota(jnp.int32, (cap, CHUNK), 0)
        yall = jnp.concatenate([y_ref[0, e] for e in range(E)], axis=0)
        for sub in range(subs):
            cols = slice(sub * CHUNK, (sub + 1) * CHUNK)
            onehot = jnp.concatenate(
                [jnp.where(ri == slot_ref[e:e + 1, cols], 1.0, 0.0).astype(BF16) for e in range(E)], axis=0)
            finish(sub, lax.dot_general(onehot, yall, nt, preferred_element_type=F32))


def _combine(starts, slot, y, x1, g, b, B, S, cap):
    T, D = x1.shape
    E = slot.shape[0]
    tm = WIDE_TILE
    tiles_per_seq = S // tm
    const = lambda shape: pl.BlockSpec(shape, lambda i, st: (0,) * len(shape), pipeline_mode=pl.Buffered(1))
    return pl.pallas_call(
        functools.partial(_combine_body, cap, tiles_per_seq),
        grid_spec=pltpu.PrefetchScalarGridSpec(
            num_scalar_prefetch=1,
            grid=(T // tm,),
            in_specs=[pl.BlockSpec((E, tm), lambda i, st: (0, i)),
                      pl.BlockSpec((1, E, cap, D), lambda i, st: (i // tiles_per_seq, 0, 0, 0)),
                      pl.BlockSpec((tm, D), lambda i, st: (i, 0)),
                      const(g.shape), const(b.shape)],
            out_specs=pl.BlockSpec((tm, D), lambda i, st: (i, 0))),
        out_shape=jax.ShapeDtypeStruct((T, D), F32),
        compiler_params=_params(("arbitrary",)),
        name="combine",
    )(starts, slot, y, x1, g, b)


def _split_hi_lo(w):
    hi = _truncate_to_bf16(w)
    return jnp.concatenate([hi.astype(BF16), (w - hi).astype(BF16)], axis=0)


def kernel(x, ln0_g, ln0_b, w_in, conv_w, attn_sink, w_attn_o, w_conv_o, w_out, ln1_g, ln1_b,
           w_router, w_gate, w_up, w_down, ln2_g, ln2_b):
    B, S, D = x.shape
    assert w_in.shape[0] == DEPTH == 1
    assert S % TOKEN_TILE == 0 and TOKEN_TILE % SUB == 0 and TOKEN_TILE % CHUNK == 0
    assert B % EXPERT_SEQS == 0 and S % WIDE_TILE == 0 and WIDE_TILE % CHUNK == 0
    T = B * S
    d_attn = N_HEADS * HEAD_DIM
    d_kv = N_KV_HEADS * HEAD_DIM
    cap = CAPACITY_FACTOR * S // N_EXPERTS
    row = lambda a: a.reshape(1, -1).astype(F32)

    splits = ((d_attn, HEAD_DIM ** -0.5 * LOG2E), (d_kv, 1.0), (d_kv, 1.0),
              (D, 1.0), (D, 1.0), (D, 1.0), (D, 1.0), (D, 1.0))
    xn, q, k, v, cb, cc, cx, ga, gc = _inproj(
        x.reshape(T, D), row(ln0_g), row(ln0_b), w_in[0].astype(BF16), splits)

    o = _attention(q, k, v, attn_sink[0].astype(F32) * LOG2E, B, S)

    x1, x1b, logits_t = _mixer_out(
        o, cb, cc, cx, ga, gc, xn, conv_w[0].astype(F32),
        w_attn_o[0].astype(BF16), w_conv_o[0].astype(BF16), w_out[0].astype(BF16),
        row(ln1_g[0]), row(ln1_b[0]), _split_hi_lo(w_router[0].T.astype(F32)), S)

    slot, aff, starts = _route(logits_t, B, S, cap)
    starts = starts.reshape(-1)

    xe, top_val = _dispatch(starts, slot, aff, x1b, B, S, cap)

    y = _experts(xe, top_val, w_gate[0], w_up[0], w_down[0])

    out = _combine(starts, slot, y, x1, row(ln2_g[0]), row(ln2_b[0]), B, S, cap)
    return out.reshape(B, S, D)
```
